```python
import jax, jax.numpy as jnp
from jax import lax
import numpy as np

D_MODEL = 2048
BATCH = 16
SEQ = 2048
DEPTH = 1

MIX_WIDTH = D_MODEL
GMLP_WIDTH = MIX_WIDTH // 2
GMLP_GROUP_DIM = 128
GMLP_GROUPS = GMLP_WIDTH // GMLP_GROUP_DIM
GMLP_CHUNK = 128
V_HEAD_DIM = 128
MLA_HEADS = (MIX_WIDTH - GMLP_WIDTH) // V_HEAD_DIM
QK_NOPE_DIM = 128
QK_ROPE_DIM = 64
QK_HEAD_DIM = QK_NOPE_DIM + QK_ROPE_DIM
Q_LORA_RANK = 512
KV_LORA_RANK = 512
ROPE_THETA = 10000.0
ATTN_BLOCK = 128
IN_COLS = 2 * GMLP_WIDTH + Q_LORA_RANK + KV_LORA_RANK + QK_ROPE_DIM
D_FF = 5504
EPS = 1e-6

kernel_name = "hymba_macaron_gmlp_mla_layer"


def rmsnorm(x, g):
    x32 = x.astype(jnp.float32)
    y = x32 * lax.rsqrt(jnp.mean(x32 * x32, axis=-1, keepdims=True) + EPS)
    return (y * g.astype(jnp.float32)).astype(x.dtype)


def swiglu(x, w_gate, w_up, w_down):
    return (jax.nn.silu(x @ w_gate) * (x @ w_up)) @ w_down


def rope_tables(positions):
    half = QK_ROPE_DIM // 2
    inv_freq = 1.0 / (ROPE_THETA ** (jnp.arange(half, dtype=jnp.float32) / half))
    ang = positions.astype(jnp.float32)[..., None] * inv_freq
    return jnp.cos(ang)[:, :, None, :], jnp.sin(ang)[:, :, None, :]


def apply_rope(x, cos, sin):
    x1, x2 = jnp.split(x.astype(jnp.float32), 2, axis=-1)
    return jnp.concatenate([x1 * cos - x2 * sin, x2 * cos + x1 * sin], axis=-1).astype(x.dtype)


def gmlp_mixer(z, v_norm_g, w_s, b_s):
    B, S, _ = z.shape
    z = jax.nn.gelu(z, approximate=False)
    u, v = jnp.split(z, 2, axis=-1)
    v = rmsnorm(v, v_norm_g)
    n_chunks = S // GMLP_CHUNK
    v = v.reshape(B, n_chunks, GMLP_CHUNK, GMLP_GROUPS, GMLP_GROUP_DIM)
    w_causal = jnp.tril(w_s)
    mixed = jnp.einsum('gts,bcsgd->bctgd', w_causal, v) + b_s.T[None, None, :, :, None]
    return u * mixed.reshape(B, S, GMLP_WIDTH)


def mla_mixer(c_q, c_kv, k_rope, cos, sin, q_norm_g, w_q_up, kv_norm_g, w_kv_up,
              q_head_g, k_head_g):
    B, S, _ = c_q.shape
    q = (rmsnorm(c_q, q_norm_g) @ w_q_up).reshape(B, S, MLA_HEADS, QK_HEAD_DIM)
    kv = (rmsnorm(c_kv, kv_norm_g) @ w_kv_up).reshape(B, S, MLA_HEADS, QK_NOPE_DIM + V_HEAD_DIM)
    k_nope, v = jnp.split(kv, [QK_NOPE_DIM], axis=-1)
    k_r = jnp.broadcast_to(k_rope[:, :, None, :], (B, S, MLA_HEADS, QK_ROPE_DIM))
    k = jnp.concatenate([k_nope, k_r], axis=-1)
    q = rmsnorm(q, q_head_g)
    k = rmsnorm(k, k_head_g)
    q = jnp.concatenate([q[..., :QK_NOPE_DIM], apply_rope(q[..., QK_NOPE_DIM:], cos, sin)], axis=-1)
    k = jnp.concatenate([k[..., :QK_NOPE_DIM], apply_rope(k[..., QK_NOPE_DIM:], cos, sin)], axis=-1)
    scale = QK_HEAD_DIM ** -0.5
    outs = []
    for i in range(S // ATTN_BLOCK):
        q_blk = q[:, i * ATTN_BLOCK:(i + 1) * ATTN_BLOCK]
        kv_len = (i + 1) * ATTN_BLOCK
        s = jnp.einsum('bqhd,bkhd->bhqk', q_blk, k[:, :kv_len]).astype(jnp.float32) * scale
        q_pos = i * ATTN_BLOCK + jnp.arange(ATTN_BLOCK)
        mask = q_pos[:, None] >= jnp.arange(kv_len)[None, :]
        s = jnp.where(mask[None, None], s, jnp.float32(-1e30))
        p = jax.nn.softmax(s, axis=-1).astype(v.dtype)
        outs.append(jnp.einsum('bhqk,bkhd->bqhd', p, v[:, :kv_len]))
    return jnp.concatenate(outs, axis=1)


def setup_inputs(seed: int = 0) -> dict:
    key = jax.random.key(seed)
    ks = jax.random.split(key, 32)
    f32 = jnp.float32

    def normal(k, shape, scale):
        return jax.random.normal(k, shape, f32) * scale

    def gain(k, shape):
        return 1.0 + 0.05 * jax.random.normal(k, shape, f32)

    L = DEPTH
    x = jax.random.normal(ks[0], (BATCH, SEQ, D_MODEL), f32)
    offsets = jax.random.randint(ks[1], (BATCH, 1), 0, 4096, dtype=jnp.int32)
    positions = (offsets + jnp.arange(SEQ, dtype=jnp.int32)[None, :]).astype(jnp.int32)
    return {
        "x": x,
        "positions": positions,
        "ffn1_norm_g": gain(ks[2], (L, D_MODEL)),
        "ffn1_w_gate": normal(ks[3], (L, D_MODEL, D_FF), D_MODEL ** -0.5),
        "ffn1_w_up": normal(ks[4], (L, D_MODEL, D_FF), D_MODEL ** -0.5),
        "ffn1_w_down": normal(ks[5], (L, D_FF, D_MODEL), D_FF ** -0.5),
        "mix_norm_g": gain(ks[6], (L, D_MODEL)),
        "w_in": normal(ks[7], (L, D_MODEL, IN_COLS), D_MODEL ** -0.5),
        "gmlp_v_norm_g": gain(ks[8], (L, GMLP_WIDTH)),
        "gmlp_w_s": normal(ks[9], (L, GMLP_GROUPS, GMLP_CHUNK, GMLP_CHUNK), 0.5 * GMLP_CHUNK ** -0.5),
        "gmlp_b_s": 1.0 + 0.1 * jax.random.normal(ks[10], (L, GMLP_GROUPS, GMLP_CHUNK), f32),
        "mla_q_norm_g": gain(ks[11], (L, Q_LORA_RANK)),
        "mla_w_q_up": normal(ks[12], (L, Q_LORA_RANK, MLA_HEADS * QK_HEAD_DIM), Q_LORA_RANK ** -0.5),
        "mla_kv_norm_g": gain(ks[13], (L, KV_LORA_RANK)),
        "mla_w_kv_up": normal(ks[14], (L, KV_LORA_RANK, MLA_HEADS * (QK_NOPE_DIM + V_HEAD_DIM)), KV_LORA_RANK ** -0.5),
        "mla_q_head_g": gain(ks[15], (L, QK_HEAD_DIM)),
        "mla_k_head_g": gain(ks[16], (L, QK_HEAD_DIM)),
        "gmlp_out_g": gain(ks[17], (L, GMLP_GROUPS, GMLP_GROUP_DIM)),
        "mla_out_g": gain(ks[18], (L, MLA_HEADS, V_HEAD_DIM)),
        "w_out": normal(ks[19], (L, MIX_WIDTH, D_MODEL), MIX_WIDTH ** -0.5),
        "ffn2_norm_g": gain(ks[20], (L, D_MODEL)),
        "ffn2_w_gate": normal(ks[21], (L, D_MODEL, D_FF), D_MODEL ** -0.5),
        "ffn2_w_up": normal(ks[22], (L, D_MODEL, D_FF), D_MODEL ** -0.5),
        "ffn2_w_down": normal(ks[23], (L, D_FF, D_MODEL), D_FF ** -0.5),
    }


def reference(x, positions, ffn1_norm_g, ffn1_w_gate, ffn1_w_up, ffn1_w_down, mix_norm_g, w_in,
              gmlp_v_norm_g, gmlp_w_s, gmlp_b_s, mla_q_norm_g, mla_w_q_up, mla_kv_norm_g,
              mla_w_kv_up, mla_q_head_g, mla_k_head_g, gmlp_out_g, mla_out_g, w_out,
              ffn2_norm_g, ffn2_w_gate, ffn2_w_up, ffn2_w_down):
    B, S, _ = x.shape
    cos, sin = rope_tables(positions)
    split_pts = [2 * GMLP_WIDTH, 2 * GMLP_WIDTH + Q_LORA_RANK,
                 2 * GMLP_WIDTH + Q_LORA_RANK + KV_LORA_RANK]
    for l in range(DEPTH):
        x = x + 0.5 * swiglu(rmsnorm(x, ffn1_norm_g[l]), ffn1_w_gate[l], ffn1_w_up[l], ffn1_w_down[l])
        h = rmsnorm(x, mix_norm_g[l])
        z = h @ w_in[l]
        z_a, c_q, c_kv, k_rope = jnp.split(z, split_pts, axis=-1)
        y_a = gmlp_mixer(z_a, gmlp_v_norm_g[l], gmlp_w_s[l], gmlp_b_s[l])
        y_a = rmsnorm(y_a.reshape(B, S, GMLP_GROUPS, GMLP_GROUP_DIM), gmlp_out_g[l])
        y_b = mla_mixer(c_q, c_kv, k_rope, cos, sin, mla_q_norm_g[l], mla_w_q_up[l],
                        mla_kv_norm_g[l], mla_w_kv_up[l], mla_q_head_g[l], mla_k_head_g[l])
        y_b = rmsnorm(y_b, mla_out_g[l])
        y = jnp.concatenate([y_a.reshape(B, S, GMLP_WIDTH), y_b.reshape(B, S, MLA_HEADS * V_HEAD_DIM)], axis=-1)
        x = x + y @ w_out[l]
        x = x + 0.5 * swiglu(rmsnorm(x, ffn2_norm_g[l]), ffn2_w_gate[l], ffn2_w_up[l], ffn2_w_down[l])
    return x
```

```python
import functools

import jax
import jax.numpy as jnp
import numpy as np
from jax import lax
from jax.experimental import pallas as pl
from jax.experimental.pallas import tpu as pltpu

D_MODEL = 2048
GMLP_WIDTH = 1024
GMLP_GROUP_DIM = 128
GMLP_GROUPS = 8
GMLP_CHUNK = 128
V_HEAD_DIM = 128
MLA_HEADS = 8
QK_NOPE_DIM = 128
QK_ROPE_DIM = 64
QK_HEAD_DIM = 192
Q_LORA_RANK = 512
KV_LORA_RANK = 512
ROPE_THETA = 10000.0
D_FF = 5504
EPS = 1e-6

LANES = 128
QK_PAD_DIM = 2 * LANES
FF_TILE = 512
D_FF_PAD = -(-D_FF // FF_TILE) * FF_TILE
VMEM_LIMIT_BYTES = 56 * 1024 * 1024

F32 = jnp.float32
BF16 = jnp.bfloat16


def _row_tile(n_rows, want):
    t = min(want, n_rows)
    assert n_rows % t == 0, (n_rows, t)
    return t


def _const_spec(shape):
    nd = len(shape)
    return pl.BlockSpec(shape, lambda *_: (0,) * nd, pipeline_mode=pl.Buffered(1))


def _rms_scale(x, width):
    ss = jnp.sum(x * x, axis=-1, keepdims=True)
    return lax.rsqrt(ss * (1.0 / width) + EPS)


def _rope_table_kernel(pos_ref, invf_ref, cos_ref, sin_ref):
    ang = pos_ref[...].astype(F32) * invf_ref[...]
    cos_ref[...] = jnp.cos(ang)
    sin_ref[...] = jnp.sin(ang)


def _rope_tables(positions):
    half = QK_ROPE_DIM // 2
    n_tok = positions.size
    per_row = LANES // half
    rows = n_tok // per_row
    inv_freq = 1.0 / (ROPE_THETA ** (jnp.arange(half, dtype=F32) / half))
    invf_row = jnp.tile(inv_freq, per_row).reshape(1, LANES)
    pos_rep = jnp.repeat(positions.reshape(-1), half).reshape(rows, LANES)
    tr = _row_tile(rows, 1024)
    cos_d, sin_d = pl.pallas_call(
        _rope_table_kernel,
        out_shape=(jax.ShapeDtypeStruct((rows, LANES), F32),) * 2,
        grid=(rows // tr,),
        in_specs=[pl.BlockSpec((tr, LANES), lambda i: (i, 0)),
                  pl.BlockSpec((1, LANES), lambda i: (0, 0))],
        out_specs=(pl.BlockSpec((tr, LANES), lambda i: (i, 0)),) * 2,
        name="rope_tables",
    )(pos_rep, invf_row)
    c = cos_d.reshape(n_tok, half)
    s = sin_d.reshape(n_tok, half)
    z = jnp.zeros((n_tok, LANES - 2 * half), F32)
    return jnp.concatenate([c, c, z], axis=1), jnp.concatenate([s, s, z], axis=1)


def _ffn_kernel(x_ref, g_ref, wg_ref, wu_ref, wd_ref, o_ref, h_ref):
    j = pl.program_id(1)

    @pl.when(j == 0)
    def _():
        x = x_ref[...]
        h_ref[...] = (x * _rms_scale(x, D_MODEL) * g_ref[...]).astype(BF16)
        o_ref[...] = x

    h = h_ref[...]
    gate = jnp.dot(h, wg_ref[...], preferred_element_type=F32)
    up = jnp.dot(h, wu_ref[...], preferred_element_type=F32)
    act = (gate / (1.0 + jnp.exp(-gate))) * (up * 0.5)
    o_ref[...] += jnp.dot(act.astype(BF16), wd_ref[...], preferred_element_type=F32)


def _ffn(x, norm_g, w_gate, w_up, w_down):
    n_tok = x.shape[0]
    tm = _row_tile(n_tok, 512)
    pad = D_FF_PAD - D_FF
    wg = jnp.pad(w_gate, ((0, 0), (0, pad))).astype(BF16)
    wu = jnp.pad(w_up, ((0, 0), (0, pad))).astype(BF16)
    wd = jnp.pad(w_down, ((0, pad), (0, 0))).astype(BF16)
    return pl.pallas_call(
        _ffn_kernel,
        out_shape=jax.ShapeDtypeStruct((n_tok, D_MODEL), F32),
        grid=(n_tok // tm, D_FF_PAD // FF_TILE),
        in_specs=[pl.BlockSpec((tm, D_MODEL), lambda i, j: (i, 0)),
                  pl.BlockSpec((1, D_MODEL), lambda i, j: (0, 0)),
                  pl.BlockSpec((D_MODEL, FF_TILE), lambda i, j: (0, j)),
                  pl.BlockSpec((D_MODEL, FF_TILE), lambda i, j: (0, j)),
                  pl.BlockSpec((FF_TILE, D_MODEL), lambda i, j: (j, 0))],
        out_specs=pl.BlockSpec((tm, D_MODEL), lambda i, j: (i, 0)),
        scratch_shapes=[pltpu.VMEM((tm, D_MODEL), BF16)],
        compiler_params=pltpu.CompilerParams(
            dimension_semantics=("parallel", "arbitrary"),
            vmem_limit_bytes=VMEM_LIMIT_BYTES),
        name="ffn",
    )(x, norm_g.reshape(1, D_MODEL), wg, wu, wd)


def _rope(r, cos_t, sin_t, lane):
    half = QK_ROPE_DIM // 2
    rot = jnp.where(lane < half, -pltpu.roll(r, LANES - half, 1), pltpu.roll(r, half, 1))
    return r * cos_t + rot * sin_t


def _mixer_front_kernel(x_ref, mixg_ref, wu_ref, wv_ref, wcq_ref, wckv_ref, wkr_ref,
                        vng_ref, ws_ref, bs_ref, og_ref,
                        qng_ref, wqup_ref, kvng_ref, wkn_ref, wvv_ref, qhg_ref, khg_ref,
                        cos_ref, sin_ref,
                        ya_ref, q_ref, k_ref, v_ref):
    tm = x_ref.shape[0]
    x = x_ref[...]
    h = (x * _rms_scale(x, D_MODEL) * mixg_ref[...]).astype(BF16)

    sqrt_half = np.float32(np.sqrt(0.5))

    def gelu(z):
        return 0.5 * z * (1.0 + lax.erf(z * sqrt_half))

    zu = gelu(jnp.dot(h, wu_ref[...], preferred_element_type=F32))
    zv = gelu(jnp.dot(h, wv_ref[...], preferred_element_type=F32))
    vn = (zv * _rms_scale(zv, GMLP_WIDTH) * vng_ref[...]).astype(BF16)
    t_idx = lax.broadcasted_iota(jnp.int32, (GMLP_CHUNK, GMLP_CHUNK), 0)
    s_idx = lax.broadcasted_iota(jnp.int32, (GMLP_CHUNK, GMLP_CHUNK), 1)
    causal = t_idx >= s_idx
    for g in range(GMLP_GROUPS):
        cols = slice(g * GMLP_GROUP_DIM, (g + 1) * GMLP_GROUP_DIM)
        w_causal = jnp.where(causal, ws_ref[g], 0.0).astype(BF16)
        for c in range(tm // GMLP_CHUNK):
            rows = slice(c * GMLP_CHUNK, (c + 1) * GMLP_CHUNK)
            mixed = jnp.dot(w_causal, vn[rows, cols], preferred_element_type=F32) + bs_ref[g]
            y = zu[rows, cols] * mixed
            y = y * _rms_scale(y, GMLP_GROUP_DIM) * og_ref[:, cols]
            ya_ref[rows, cols] = y.astype(BF16)

    cos_t = cos_ref[...]
    sin_t = sin_ref[...]
    lane = lax.broadcasted_iota(jnp.int32, (tm, LANES), 1)
    scale = np.float32(QK_HEAD_DIM ** -0.5)

    cq = jnp.dot(h, wcq_ref[...], preferred_element_type=F32)
    cqn = (cq * _rms_scale(cq, Q_LORA_RANK) * qng_ref[...]).astype(BF16)
    qf = jnp.dot(cqn, wqup_ref[...], preferred_element_type=F32)
    qhg_n = qhg_ref[:, :LANES]
    qhg_r = qhg_ref[:, LANES:]
    for hd in range(MLA_HEADS):
        base = hd * QK_PAD_DIM
        qn = qf[:, base:base + LANES]
        qr = qf[:, base + LANES:base + QK_PAD_DIM]
        ss = jnp.sum(qn * qn, axis=-1, keepdims=True) + jnp.sum(qr * qr, axis=-1, keepdims=True)
        inv = lax.rsqrt(ss * (1.0 / QK_HEAD_DIM) + EPS) * scale
        q_ref[:, base:base + LANES] = (qn * inv * qhg_n).astype(BF16)
        q_ref[:, base + LANES:base + QK_PAD_DIM] = _rope(
            qr * inv * qhg_r, cos_t, sin_t, lane).astype(BF16)

    ckv = jnp.dot(h, wckv_ref[...], preferred_element_type=F32)
    ckvn = (ckv * _rms_scale(ckv, KV_LORA_RANK) * kvng_ref[...]).astype(BF16)
    kn_all = jnp.dot(ckvn, wkn_ref[...], preferred_element_type=F32)
    v_ref[...] = jnp.dot(ckvn, wvv_ref[...], preferred_element_type=F32).astype(BF16)
    kr = jnp.dot(h, wkr_ref[...], preferred_element_type=F32)
    kr_ss = jnp.sum(kr * kr, axis=-1, keepdims=True)
    khg_n = khg_ref[:, :LANES]
    khg_r = khg_ref[:, LANES:]
    for hd in range(MLA_HEADS):
        base = hd * QK_PAD_DIM
        kn = kn_all[:, hd * LANES:(hd + 1) * LANES]
        ss = jnp.sum(kn * kn, axis=-1, keepdims=True) + kr_ss
        inv = lax.rsqrt(ss * (1.0 / QK_HEAD_DIM) + EPS)
        k_ref[:, base:base + LANES] = (kn * inv * khg_n).astype(BF16)
        k_ref[:, base + LANES:base + QK_PAD_DIM] = _rope(
            kr * inv * khg_r, cos_t, sin_t, lane).astype(BF16)


def _mixer_front(x, cos_tab, sin_tab, mix_norm_g, w_in, v_norm_g, w_s, b_s, out_g,
                 q_norm_g, w_q_up, kv_norm_g, w_kv_up, q_head_g, k_head_g):
    n_tok = x.shape[0]
    tm = _row_tile(n_tok, 256)
    H = MLA_HEADS
    c0 = GMLP_WIDTH
    c1 = 2 * GMLP_WIDTH
    c2 = c1 + Q_LORA_RANK
    c3 = c2 + KV_LORA_RANK
    wu = w_in[:, :c0].astype(BF16)
    wv = w_in[:, c0:c1].astype(BF16)
    wcq = w_in[:, c1:c2].astype(BF16)
    wckv = w_in[:, c2:c3].astype(BF16)
    wkr = jnp.pad(w_in[:, c3:], ((0, 0), (0, LANES - QK_ROPE_DIM))).astype(BF16)
    zpad = QK_PAD_DIM - QK_HEAD_DIM
    wqup = jnp.pad(w_q_up.reshape(Q_LORA_RANK, H, QK_HEAD_DIM), ((0, 0), (0, 0), (0, zpad)))
    wqup = wqup.reshape(Q_LORA_RANK, H * QK_PAD_DIM).astype(BF16)
    wkv = w_kv_up.reshape(KV_LORA_RANK, H, QK_NOPE_DIM + V_HEAD_DIM)
    wkn = wkv[:, :, :QK_NOPE_DIM].reshape(KV_LORA_RANK, H * QK_NOPE_DIM).astype(BF16)
    wvv = wkv[:, :, QK_NOPE_DIM:].reshape(KV_LORA_RANK, H * V_HEAD_DIM).astype(BF16)
    qhg = jnp.pad(q_head_g, (0, zpad)).reshape(1, QK_PAD_DIM)
    khg = jnp.pad(k_head_g, (0, zpad)).reshape(1, QK_PAD_DIM)
    bs_full = jnp.broadcast_to(b_s[:, :, None], (GMLP_GROUPS, GMLP_CHUNK, GMLP_GROUP_DIM))

    row = lambda w: pl.BlockSpec((tm, w), lambda i: (i, 0))
    consts = [mix_norm_g.reshape(1, D_MODEL), wu, wv, wcq, wckv, wkr,
              v_norm_g.reshape(1, GMLP_WIDTH), w_s, bs_full, out_g.reshape(1, GMLP_WIDTH),
              q_norm_g.reshape(1, Q_LORA_RANK), wqup, kv_norm_g.reshape(1, KV_LORA_RANK),
              wkn, wvv, qhg, khg]
    return pl.pallas_call(
        _mixer_front_kernel,
        out_shape=(jax.ShapeDtypeStruct((n_tok, GMLP_WIDTH), BF16),
                   jax.ShapeDtypeStruct((n_tok, H * QK_PAD_DIM), BF16),
                   jax.ShapeDtypeStruct((n_tok, H * QK_PAD_DIM), BF16),
                   jax.ShapeDtypeStruct((n_tok, H * V_HEAD_DIM), BF16)),
        grid=(n_tok // tm,),
        in_specs=[row(D_MODEL)] + [_const_spec(a.shape) for a in consts] + [row(LANES), row(LANES)],
        out_specs=(row(GMLP_WIDTH), row(H * QK_PAD_DIM), row(H * QK_PAD_DIM), row(H * V_HEAD_DIM)),
        compiler_params=pltpu.CompilerParams(
            dimension_semantics=("parallel",), vmem_limit_bytes=VMEM_LIMIT_BYTES),
        name="mixer_front",
    )(x, *consts, cos_tab, sin_tab)


def _attn_kernel(q_ref, k_ref, v_ref, g_ref, o_ref):
    tq = q_ref.shape[0]
    i = pl.program_id(2)
    q = q_ref[...]

    def step(j, carry, masked):
        m, l, acc = carry
        start = pl.multiple_of(j * tq, tq)
        kj = k_ref[pl.ds(start, tq), :]
        vj = v_ref[pl.ds(start, tq), :]
        s = lax.dot_general(q, kj, (((1,), (1,)), ((), ())), preferred_element_type=F32)
        if masked:
            qi = lax.broadcasted_iota(jnp.int32, (tq, tq), 0)
            ki = lax.broadcasted_iota(jnp.int32, (tq, tq), 1)
            s = jnp.where(qi >= ki, s, F32(-1e30))
        m_new = jnp.maximum(m, jnp.max(s, axis=-1, keepdims=True))
        alpha = jnp.exp(m - m_new)
        p = jnp.exp(s - m_new)
        l = alpha * l + jnp.sum(p, axis=-1, keepdims=True)
        acc = alpha * acc + jnp.dot(p.astype(BF16), vj, preferred_element_type=F32)
        return m_new, l, acc

    init = (jnp.full((tq, 1), -1e30, F32), jnp.zeros((tq, 1), F32),
            jnp.zeros((tq, V_HEAD_DIM), F32))
    carry = lax.fori_loop(0, i, lambda j, c: step(j, c, False), init)
    _, l, acc = step(i, carry, True)
    o = acc / l
    o_ref[...] = (o * _rms_scale(o, V_HEAD_DIM) * g_ref[...]).astype(BF16)


def _attention(q, k, v, out_g, batch, seq):
    H = MLA_HEADS
    tq = _row_tile(seq, 512)
    nq = seq // tq
    return pl.pallas_call(
        _attn_kernel,
        out_shape=jax.ShapeDtypeStruct((batch * seq, H * V_HEAD_DIM), BF16),
        grid=(batch, H, nq),
        in_specs=[pl.BlockSpec((tq, QK_PAD_DIM), lambda b, h, i: (b * nq + i, h)),
                  pl.BlockSpec((seq, QK_PAD_DIM), lambda b, h, i: (b, h)),
                  pl.BlockSpec((seq, V_HEAD_DIM), lambda b, h, i: (b, h)),
                  pl.BlockSpec((1, V_HEAD_DIM), lambda b, h, i: (0, h))],
        out_specs=pl.BlockSpec((tq, V_HEAD_DIM), lambda b, h, i: (b * nq + i, h)),
        compiler_params=pltpu.CompilerParams(
            dimension_semantics=("parallel", "parallel", "arbitrary"),
            vmem_limit_bytes=VMEM_LIMIT_BYTES),
        name="attention",
    )(q, k, v, out_g.reshape(1, H * V_HEAD_DIM))


def _out_proj_kernel(x_ref, ya_ref, yb_ref, wa_ref, wb_ref, o_ref):
    o_ref[...] = (x_ref[...]
                  + jnp.dot(ya_ref[...], wa_ref[...], preferred_element_type=F32)
                  + jnp.dot(yb_ref[...], wb_ref[...], preferred_element_type=F32))


def _out_proj(x, ya, yb, w_out):
    n_tok = x.shape[0]
    tm = _row_tile(n_tok, 512)
    wa = w_out[:GMLP_WIDTH].astype(BF16)
    wb = w_out[GMLP_WIDTH:].astype(BF16)
    row = lambda w: pl.BlockSpec((tm, w), lambda i: (i, 0))
    return pl.pallas_call(
        _out_proj_kernel,
        out_shape=jax.ShapeDtypeStruct((n_tok, D_MODEL), F32),
        grid=(n_tok // tm,),
        in_specs=[row(D_MODEL), row(GMLP_WIDTH), row(MLA_HEADS * V_HEAD_DIM),
                  _const_spec(wa.shape), _const_spec(wb.shape)],
        out_specs=row(D_MODEL),
        compiler_params=pltpu.CompilerParams(
            dimension_semantics=("parallel",), vmem_limit_bytes=VMEM_LIMIT_BYTES),
        name="out_proj",
    )(x, ya, yb, wa, wb)


def kernel(x, positions, ffn1_norm_g, ffn1_w_gate, ffn1_w_up, ffn1_w_down, mix_norm_g, w_in, gmlp_v_norm_g, gmlp_w_s, gmlp_b_s, mla_q_norm_g, mla_w_q_up, mla_kv_norm_g, mla_w_kv_up, mla_q_head_g, mla_k_head_g, gmlp_out_g, mla_out_g, w_out, ffn2_norm_g, ffn2_w_gate, ffn2_w_up, ffn2_w_down):
    B, S, D = x.shape
    assert D == D_MODEL and S % GMLP_CHUNK == 0
    depth = ffn1_norm_g.shape[0]
    cos_tab, sin_tab = _rope_tables(positions)
    xf = x.reshape(B * S, D)
    for l in range(depth):
        xf = _ffn(xf, ffn1_norm_g[l], ffn1_w_gate[l], ffn1_w_up[l], ffn1_w_down[l])
        ya, q, k, v = _mixer_front(
            xf, cos_tab, sin_tab, mix_norm_g[l], w_in[l], gmlp_v_norm_g[l], gmlp_w_s[l],
            gmlp_b_s[l], gmlp_out_g[l], mla_q_norm_g[l], mla_w_q_up[l], mla_kv_norm_g[l],
            mla_w_kv_up[l], mla_q_head_g[l], mla_k_head_g[l])
        yb = _attention(q, k, v, mla_out_g[l], B, S)
        xf = _out_proj(xf, ya, yb, w_out[l])
        xf = _ffn(xf, ffn2_norm_g[l], ffn2_w_gate[l], ffn2_w_up[l], ffn2_w_down[l])
    return xf.reshape(B, S, D)
```

```python
import functools

import jax
import jax.numpy as jnp
import numpy as np
from jax import lax
from jax.experimental import pallas as pl
from jax.experimental.pallas import tpu as pltpu

D_MODEL = 2048
GMLP_WIDTH = 1024
GMLP_GROUP_DIM = 128
GMLP_GROUPS = 8
GMLP_CHUNK = 128
V_HEAD_DIM = 128
MLA_HEADS = 8
QK_NOPE_DIM = 128
QK_ROPE_DIM = 64
QK_HEAD_DIM = 192
Q_LORA_RANK = 512
KV_LORA_RANK = 512
ROPE_THETA = 10000.0
D_FF = 5504
EPS = 1e-6

LANES = 128
QK_PAD_DIM = 2 * LANES
FF_TILE = 512
D_FF_PAD = -(-D_FF // FF_TILE) * FF_TILE
FFN_ROW_TILE = 1024
ATTN_Q_TILE = 256
VMEM_LIMIT_BYTES = 60 * 1024 * 1024
LOG2E = 1.4426950408889634

F32 = jnp.float32
BF16 = jnp.bfloat16


def _row_tile(n_rows, want):
    t = min(want, n_rows)
    assert n_rows % t == 0, (n_rows, t)
    return t


def _const_spec(shape):
    nd = len(shape)
    return pl.BlockSpec(shape, lambda *_: (0,) * nd, pipeline_mode=pl.Buffered(1))


def _rms_scale(x, width):
    ss = jnp.sum(x * x, axis=-1, keepdims=True)
    return lax.rsqrt(ss * (1.0 / width) + EPS)


def _rope_table_kernel(pos_ref, invf_ref, cos_ref, sin_ref):
    half = QK_ROPE_DIM // 2
    ang = invf_ref[...] * pos_ref[...].astype(F32)
    c = jnp.cos(ang)
    s = jnp.sin(ang)
    zeros = jnp.zeros((LANES - 2 * half, ang.shape[1]), F32)
    for ref, t in ((cos_ref, c), (sin_ref, s)):
        ref[0:half, :] = t
        ref[half:2 * half, :] = t
        ref[2 * half:, :] = zeros


def _rope_tables(positions):
    half = QK_ROPE_DIM // 2
    n_tok = positions.size
    inv_freq = 1.0 / (ROPE_THETA ** (jnp.arange(half, dtype=F32) / half))
    tk = _row_tile(n_tok, 2048)
    return pl.pallas_call(
        _rope_table_kernel,
        out_shape=(jax.ShapeDtypeStruct((LANES, n_tok), F32),) * 2,
        grid=(n_tok // tk,),
        in_specs=[pl.BlockSpec((1, tk), lambda i: (0, i)),
                  pl.BlockSpec((half, 1), lambda i: (0, 0))],
        out_specs=(pl.BlockSpec((LANES, tk), lambda i: (0, i)),) * 2,
        name="rope_tables",
    )(positions.reshape(1, n_tok), inv_freq.reshape(half, 1))


def _ffn_kernel(x_ref, g_ref, wg_ref, wu_ref, wd_ref, o_ref, h_ref):
    j = pl.program_id(1)

    @pl.when(j == 0)
    def _():
        x = x_ref[...]
        h_ref[...] = (x * _rms_scale(x, D_MODEL) * g_ref[...]).astype(BF16)
        o_ref[...] = x

    h = h_ref[...]
    gate = jnp.dot(h, wg_ref[...], preferred_element_type=F32)
    up = jnp.dot(h, wu_ref[...], preferred_element_type=F32)
    act = (gate / (1.0 + jnp.exp(-gate))) * (up * 0.5)
    o_ref[...] += jnp.dot(act.astype(BF16), wd_ref[...], preferred_element_type=F32)


def _ffn(x, norm_g, w_gate, w_up, w_down):
    n_tok = x.shape[0]
    tm = _row_tile(n_tok, FFN_ROW_TILE)
    pad = D_FF_PAD - D_FF
    wg = jnp.pad(w_gate.astype(BF16), ((0, 0), (0, pad)))
    wu = jnp.pad(w_up.astype(BF16), ((0, 0), (0, pad)))
    wd = jnp.pad(w_down.astype(BF16), ((0, pad), (0, 0)))
    return pl.pallas_call(
        _ffn_kernel,
        out_shape=jax.ShapeDtypeStruct((n_tok, D_MODEL), F32),
        grid=(n_tok // tm, D_FF_PAD // FF_TILE),
        in_specs=[pl.BlockSpec((tm, D_MODEL), lambda i, j: (i, 0)),
                  pl.BlockSpec((1, D_MODEL), lambda i, j: (0, 0)),
                  pl.BlockSpec((D_MODEL, FF_TILE), lambda i, j: (0, j)),
                  pl.BlockSpec((D_MODEL, FF_TILE), lambda i, j: (0, j)),
                  pl.BlockSpec((FF_TILE, D_MODEL), lambda i, j: (j, 0))],
        out_specs=pl.BlockSpec((tm, D_MODEL), lambda i, j: (i, 0)),
        scratch_shapes=[pltpu.VMEM((tm, D_MODEL), BF16)],
        compiler_params=pltpu.CompilerParams(
            dimension_semantics=("parallel", "arbitrary"),
            vmem_limit_bytes=VMEM_LIMIT_BYTES),
        name="ffn",
    )(x, norm_g.reshape(1, D_MODEL), wg, wu, wd)


def _rope(r, cos_t, sin_t, lane):
    half = QK_ROPE_DIM // 2
    rot = jnp.where(lane < half, -pltpu.roll(r, LANES - half, 1), pltpu.roll(r, half, 1))
    return r * cos_t + rot * sin_t


def _mixer_front_kernel(x_ref, mixg_ref, wu_ref, wv_ref, wcq_ref, wckv_ref, wkr_ref,
                        vng_ref, ws_ref, bs_ref, og_ref,
                        qng_ref, wqup_ref, kvng_ref, wkn_ref, wvv_ref, qhg_ref, khg_ref,
                        cos_ref, sin_ref,
                        ya_ref, q_ref, k_ref, vt_ref):
    tm = x_ref.shape[0]
    x = x_ref[...]
    h = (x * _rms_scale(x, D_MODEL) * mixg_ref[...]).astype(BF16)

    sqrt_half = np.float32(np.sqrt(0.5))

    def gelu(z):
        return 0.5 * z * (1.0 + lax.erf(z * sqrt_half))

    zu = gelu(jnp.dot(h, wu_ref[...], preferred_element_type=F32))
    zv = gelu(jnp.dot(h, wv_ref[...], preferred_element_type=F32))
    vn = (zv * _rms_scale(zv, GMLP_WIDTH) * vng_ref[...]).astype(BF16)
    t_idx = lax.broadcasted_iota(jnp.int32, (GMLP_CHUNK, GMLP_CHUNK), 0)
    s_idx = lax.broadcasted_iota(jnp.int32, (GMLP_CHUNK, GMLP_CHUNK), 1)
    causal = t_idx >= s_idx
    for g in range(GMLP_GROUPS):
        cols = slice(g * GMLP_GROUP_DIM, (g + 1) * GMLP_GROUP_DIM)
        w_causal = jnp.where(causal, ws_ref[g], 0.0).astype(BF16)
        for c in range(tm // GMLP_CHUNK):
            rows = slice(c * GMLP_CHUNK, (c + 1) * GMLP_CHUNK)
            mixed = jnp.dot(w_causal, vn[rows, cols], preferred_element_type=F32) + bs_ref[g]
            y = zu[rows, cols] * mixed
            y = y * _rms_scale(y, GMLP_GROUP_DIM) * og_ref[:, cols]
            ya_ref[rows, cols] = y.astype(BF16)

    cos_t = cos_ref[...].T
    sin_t = sin_ref[...].T
    lane = lax.broadcasted_iota(jnp.int32, (tm, LANES), 1)
    scale = np.float32(QK_HEAD_DIM ** -0.5 * LOG2E)

    cq = jnp.dot(h, wcq_ref[...], preferred_element_type=F32)
    cqn = (cq * _rms_scale(cq, Q_LORA_RANK) * qng_ref[...]).astype(BF16)
    qf = jnp.dot(cqn, wqup_ref[...], preferred_element_type=F32)
    qhg_n = qhg_ref[:, :LANES]
    qhg_r = qhg_ref[:, LANES:]
    for hd in range(MLA_HEADS):
        base = hd * QK_PAD_DIM
        qn = qf[:, base:base + LANES]
        qr = qf[:, base + LANES:base + QK_PAD_DIM]
        ss = jnp.sum(qn * qn, axis=-1, keepdims=True) + jnp.sum(qr * qr, axis=-1, keepdims=True)
        inv = lax.rsqrt(ss * (1.0 / QK_HEAD_DIM) + EPS) * scale
        q_ref[:, base:base + LANES] = (qn * inv * qhg_n).astype(BF16)
        q_ref[:, base + LANES:base + QK_PAD_DIM] = _rope(
            qr * inv * qhg_r, cos_t, sin_t, lane).astype(BF16)

    ckv = jnp.dot(h, wckv_ref[...], preferred_element_type=F32)
    ckvn = (ckv * _rms_scale(ckv, KV_LORA_RANK) * kvng_ref[...]).astype(BF16)
    kn_all = jnp.dot(ckvn, wkn_ref[...], preferred_element_type=F32)
    vt_ref[...] = jnp.dot(ckvn, wvv_ref[...], preferred_element_type=F32).T.astype(BF16)
    kr = jnp.dot(h, wkr_ref[...], preferred_element_type=F32)
    kr_ss = jnp.sum(kr * kr, axis=-1, keepdims=True)
    khg_n = khg_ref[:, :LANES]
    khg_r = khg_ref[:, LANES:]
    for hd in range(MLA_HEADS):
        base = hd * QK_PAD_DIM
        kn = kn_all[:, hd * LANES:(hd + 1) * LANES]
        ss = jnp.sum(kn * kn, axis=-1, keepdims=True) + kr_ss
        inv = lax.rsqrt(ss * (1.0 / QK_HEAD_DIM) + EPS)
        k_ref[:, base:base + LANES] = (kn * inv * khg_n).astype(BF16)
        k_ref[:, base + LANES:base + QK_PAD_DIM] = _rope(
            kr * inv * khg_r, cos_t, sin_t, lane).astype(BF16)


def _mixer_front(x, cos_tab, sin_tab, mix_norm_g, w_in, v_norm_g, w_s, b_s, out_g,
                 q_norm_g, w_q_up, kv_norm_g, w_kv_up, q_head_g, k_head_g):
    n_tok = x.shape[0]
    tm = _row_tile(n_tok, 256)
    H = MLA_HEADS
    c0 = GMLP_WIDTH
    c1 = 2 * GMLP_WIDTH
    c2 = c1 + Q_LORA_RANK
    c3 = c2 + KV_LORA_RANK
    wu = w_in[:, :c0].astype(BF16)
    wv = w_in[:, c0:c1].astype(BF16)
    wcq = w_in[:, c1:c2].astype(BF16)
    wckv = w_in[:, c2:c3].astype(BF16)
    wkr = jnp.pad(w_in[:, c3:], ((0, 0), (0, LANES - QK_ROPE_DIM))).astype(BF16)
    zpad = QK_PAD_DIM - QK_HEAD_DIM
    wqup = jnp.pad(w_q_up.reshape(Q_LORA_RANK, H, QK_HEAD_DIM), ((0, 0), (0, 0), (0, zpad)))
    wqup = wqup.reshape(Q_LORA_RANK, H * QK_PAD_DIM).astype(BF16)
    wkv = w_kv_up.reshape(KV_LORA_RANK, H, QK_NOPE_DIM + V_HEAD_DIM)
    wkn = wkv[:, :, :QK_NOPE_DIM].reshape(KV_LORA_RANK, H * QK_NOPE_DIM).astype(BF16)
    wvv = wkv[:, :, QK_NOPE_DIM:].reshape(KV_LORA_RANK, H * V_HEAD_DIM).astype(BF16)
    qhg = jnp.pad(q_head_g, (0, zpad)).reshape(1, QK_PAD_DIM)
    khg = jnp.pad(k_head_g, (0, zpad)).reshape(1, QK_PAD_DIM)
    bs_full = jnp.broadcast_to(b_s[:, :, None], (GMLP_GROUPS, GMLP_CHUNK, GMLP_GROUP_DIM))

    row = lambda w: pl.BlockSpec((tm, w), lambda i: (i, 0))
    col = lambda h: pl.BlockSpec((h, tm), lambda i: (0, i))
    consts = [mix_norm_g.reshape(1, D_MODEL), wu, wv, wcq, wckv, wkr,
              v_norm_g.reshape(1, GMLP_WIDTH), w_s, bs_full, out_g.reshape(1, GMLP_WIDTH),
              q_norm_g.reshape(1, Q_LORA_RANK), wqup, kv_norm_g.reshape(1, KV_LORA_RANK),
              wkn, wvv, qhg, khg]
    return pl.pallas_call(
        _mixer_front_kernel,
        out_shape=(jax.ShapeDtypeStruct((n_tok, GMLP_WIDTH), BF16),
                   jax.ShapeDtypeStruct((n_tok, H * QK_PAD_DIM), BF16),
                   jax.ShapeDtypeStruct((n_tok, H * QK_PAD_DIM), BF16),
                   jax.ShapeDtypeStruct((H * V_HEAD_DIM, n_tok), BF16)),
        grid=(n_tok // tm,),
        in_specs=[row(D_MODEL)] + [_const_spec(a.shape) for a in consts] + [col(LANES), col(LANES)],
        out_specs=(row(GMLP_WIDTH), row(H * QK_PAD_DIM), row(H * QK_PAD_DIM), col(H * V_HEAD_DIM)),
        compiler_params=pltpu.CompilerParams(
            dimension_semantics=("parallel",), vmem_limit_bytes=VMEM_LIMIT_BYTES),
        name="mixer_front",
    )(x, *consts, cos_tab, sin_tab)


def _attn_kernel(q_ref, k_ref, vt_ref, g_ref, o_ref):
    seq = q_ref.shape[0]
    tq = min(ATTN_Q_TILE, seq)
    nt = (((1,), (1,)), ((), ()))
    key_i = lax.broadcasted_iota(jnp.int32, (tq, tq), 0)
    qry_i = lax.broadcasted_iota(jnp.int32, (tq, tq), 1)
    causal = key_i <= qry_i
    g = g_ref[...]
    for i in range(seq // tq):
        lo, hi = i * tq, (i + 1) * tq
        q = q_ref[lo:hi, :]
        s_d = lax.dot_general(k_ref[lo:hi, :], q, nt, preferred_element_type=F32)
        s_d = jnp.where(causal, s_d, F32(-1e30))
        m = jnp.max(s_d, axis=0, keepdims=True)
        if i > 0:
            s_o = lax.dot_general(k_ref[0:lo, :], q, nt, preferred_element_type=F32)
            m = jnp.maximum(m, jnp.max(s_o, axis=0, keepdims=True))
        p_d = jnp.exp2(s_d - m)
        l = jnp.sum(p_d, axis=0, keepdims=True)
        o_t = jnp.dot(vt_ref[:, lo:hi], p_d.astype(BF16), preferred_element_type=F32)
        if i > 0:
            p_o = jnp.exp2(s_o - m)
            l = l + jnp.sum(p_o, axis=0, keepdims=True)
            o_t = o_t + jnp.dot(vt_ref[:, 0:lo], p_o.astype(BF16), preferred_element_type=F32)
        o_t = o_t / l
        ms = jnp.sum(o_t * o_t, axis=0, keepdims=True) * (1.0 / V_HEAD_DIM)
        o_ref[lo:hi, :] = ((o_t * lax.rsqrt(ms + EPS)).T * g).astype(BF16)


def _attention(q, k, vt, out_g, batch, seq):
    H = MLA_HEADS
    return pl.pallas_call(
        _attn_kernel,
        out_shape=jax.ShapeDtypeStruct((batch * seq, H * V_HEAD_DIM), BF16),
        grid=(batch, H),
        in_specs=[pl.BlockSpec((seq, QK_PAD_DIM), lambda b, h: (b, h)),
                  pl.BlockSpec((seq, QK_PAD_DIM), lambda b, h: (b, h)),
                  pl.BlockSpec((V_HEAD_DIM, seq), lambda b, h: (h, b)),
                  pl.BlockSpec((1, V_HEAD_DIM), lambda b, h: (0, h))],
        out_specs=pl.BlockSpec((seq, V_HEAD_DIM), lambda b, h: (b, h)),
        compiler_params=pltpu.CompilerParams(
            dimension_semantics=("parallel", "parallel"),
            vmem_limit_bytes=VMEM_LIMIT_BYTES),
        name="attention",
    )(q, k, vt, out_g.reshape(1, H * V_HEAD_DIM))


def _out_proj_kernel(x_ref, ya_ref, yb_ref, wa_ref, wb_ref, o_ref):
    o_ref[...] = (x_ref[...]
                  + jnp.dot(ya_ref[...], wa_ref[...], preferred_element_type=F32)
                  + jnp.dot(yb_ref[...], wb_ref[...], preferred_element_type=F32))


def _out_proj(x, ya, yb, w_out):
    n_tok = x.shape[0]
    tm = _row_tile(n_tok, 512)
    wa = w_out[:GMLP_WIDTH].astype(BF16)
    wb = w_out[GMLP_WIDTH:].astype(BF16)
    row = lambda w: pl.BlockSpec((tm, w), lambda i: (i, 0))
    return pl.pallas_call(
        _out_proj_kernel,
        out_shape=jax.ShapeDtypeStruct((n_tok, D_MODEL), F32),
        grid=(n_tok // tm,),
        in_specs=[row(D_MODEL), row(GMLP_WIDTH), row(MLA_HEADS * V_HEAD_DIM),
                  _const_spec(wa.shape), _const_spec(wb.shape)],
        out_specs=row(D_MODEL),
        compiler_params=pltpu.CompilerParams(
            dimension_semantics=("parallel",), vmem_limit_bytes=VMEM_LIMIT_BYTES),
        name="out_proj",
    )(x, ya, yb, wa, wb)


def kernel(x, positions, ffn1_norm_g, ffn1_w_gate, ffn1_w_up, ffn1_w_down, mix_norm_g, w_in, gmlp_v_norm_g, gmlp_w_s, gmlp_b_s, mla_q_norm_g, mla_w_q_up, mla_kv_norm_g, mla_w_kv_up, mla_q_head_g, mla_k_head_g, gmlp_out_g, mla_out_g, w_out, ffn2_norm_g, ffn2_w_gate, ffn2_w_up, ffn2_w_down):
    B, S, D = x.shape
    assert D == D_MODEL and S % GMLP_CHUNK == 0
    depth = ffn1_norm_g.shape[0]
    cos_tab, sin_tab = _rope_tables(positions)
    xf = x.reshape(B * S, D)
    for l in range(depth):
        xf = _ffn(xf, ffn1_norm_g[l], ffn1_w_gate[l], ffn1_w_up[l], ffn1_w_down[l])
        ya, q, k, vt = _mixer_front(
            xf, cos_tab, sin_tab, mix_norm_g[l], w_in[l], gmlp_v_norm_g[l], gmlp_w_s[l],
            gmlp_b_s[l], gmlp_out_g[l], mla_q_norm_g[l], mla_w_q_up[l], mla_kv_norm_g[l],
            mla_w_kv_up[l], mla_q_head_g[l], mla_k_head_g[l])
        yb = _attention(q, k, vt, mla_out_g[l], B, S)
        xf = _out_proj(xf, ya, yb, w_out[l])
        xf = _ffn(xf, ffn2_norm_g[l], ffn2_w_gate[l], ffn2_w_up[l], ffn2_w_down[l])
    return xf.reshape(B, S, D)
```

```python
import functools

import jax
import jax.numpy as jnp
import numpy as np
from jax import lax
from jax.experimental import pallas as pl
from jax.experimental.pallas import tpu as pltpu

D_MODEL = 2048
GMLP_WIDTH = 1024
GMLP_GROUP_DIM = 128
GMLP_GROUPS = 8
GMLP_CHUNK = 128
V_HEAD_DIM = 128
MLA_HEADS = 8
QK_NOPE_DIM = 128
QK_ROPE_DIM = 64
QK_HEAD_DIM = 192
Q_LORA_RANK = 512
KV_LORA_RANK = 512
ROPE_THETA = 10000.0
D_FF = 5504
EPS = 1e-6

LANES = 128
QK_PAD_DIM = 2 * LANES
FF_TILE = 512
FF_STEPS = -(-D_FF // FF_TILE)
FF_TAIL = D_FF - (FF_STEPS - 1) * FF_TILE
FFN_ROW_TILE = 1024
MIXER_ROW_TILE = 256
OUT_PROJ_ROW_TILE = 512
ATTN_Q_TILE = 256
ATTN_SCORES_AHEAD = 2
VMEM_LIMIT_BYTES = 60 * 1024 * 1024
LOG2E = 1.4426950408889634

F32 = jnp.float32
BF16 = jnp.bfloat16


def _row_tile(n_rows, want):
    t = min(want, n_rows)
    assert n_rows % t == 0, (n_rows, t)
    return t


def _const_spec(shape):
    nd = len(shape)
    return pl.BlockSpec(shape, lambda *_: (0,) * nd, pipeline_mode=pl.Buffered(1))


def _rms_scale(x, width):
    ss = jnp.sum(x * x, axis=-1, keepdims=True)
    return lax.rsqrt(ss * (1.0 / width) + EPS)


def _rope_table_kernel(pos_ref, invf_ref, cos_ref, sin_ref):
    half = QK_ROPE_DIM // 2
    ang = invf_ref[...] * pos_ref[...].astype(F32)
    c = jnp.cos(ang)
    s = jnp.sin(ang)
    zeros = jnp.zeros((LANES - 2 * half, ang.shape[1]), F32)
    for ref, t in ((cos_ref, c), (sin_ref, s)):
        ref[0:half, :] = t
        ref[half:2 * half, :] = t
        ref[2 * half:, :] = zeros


def _rope_tables(positions):
    half = QK_ROPE_DIM // 2
    n_tok = positions.size
    inv_freq = 1.0 / (ROPE_THETA ** (jnp.arange(half, dtype=F32) / half))
    tk = _row_tile(n_tok, 2048)
    return pl.pallas_call(
        _rope_table_kernel,
        out_shape=(jax.ShapeDtypeStruct((LANES, n_tok), F32),) * 2,
        grid=(n_tok // tk,),
        in_specs=[pl.BlockSpec((1, tk), lambda i: (0, i)),
                  pl.BlockSpec((half, 1), lambda i: (0, 0))],
        out_specs=(pl.BlockSpec((LANES, tk), lambda i: (0, i)),) * 2,
        name="rope_tables",
    )(positions.reshape(1, n_tok), inv_freq.reshape(half, 1))


def _ffn_kernel(x_ref, g_ref, wg_ref, wu_ref, wd_ref, o_ref, h_ref):
    j = pl.program_id(1)

    @pl.when(j == 0)
    def _():
        x = x_ref[...]
        h_ref[...] = (x * _rms_scale(x, D_MODEL) * g_ref[...]).astype(BF16)
        o_ref[...] = x

    def step(width):
        h = h_ref[...]
        gate = jnp.dot(h, wg_ref[:, :width], preferred_element_type=F32)
        up = jnp.dot(h, wu_ref[:, :width], preferred_element_type=F32)
        act = (gate / (1.0 + jnp.exp(-gate))) * (up * 0.5)
        o_ref[...] += jnp.dot(act.astype(BF16), wd_ref[:width, :], preferred_element_type=F32)

    @pl.when(j < FF_STEPS - 1)
    def _():
        step(FF_TILE)

    @pl.when(j == FF_STEPS - 1)
    def _():
        step(FF_TAIL)


def _ffn(x, norm_g, w_gate, w_up, w_down):
    n_tok = x.shape[0]
    tm = _row_tile(n_tok, FFN_ROW_TILE)
    wg = w_gate.astype(BF16)
    wu = w_up.astype(BF16)
    wd = w_down.astype(BF16)
    return pl.pallas_call(
        _ffn_kernel,
        out_shape=jax.ShapeDtypeStruct((n_tok, D_MODEL), F32),
        grid=(n_tok // tm, FF_STEPS),
        in_specs=[pl.BlockSpec((tm, D_MODEL), lambda i, j: (i, 0)),
                  pl.BlockSpec((1, D_MODEL), lambda i, j: (0, 0)),
                  pl.BlockSpec((D_MODEL, FF_TILE), lambda i, j: (0, j)),
                  pl.BlockSpec((D_MODEL, FF_TILE), lambda i, j: (0, j)),
                  pl.BlockSpec((FF_TILE, D_MODEL), lambda i, j: (j, 0))],
        out_specs=pl.BlockSpec((tm, D_MODEL), lambda i, j: (i, 0)),
        scratch_shapes=[pltpu.VMEM((tm, D_MODEL), BF16)],
        compiler_params=pltpu.CompilerParams(
            dimension_semantics=("parallel", "arbitrary"),
            vmem_limit_bytes=VMEM_LIMIT_BYTES),
        name="ffn",
    )(x, norm_g.reshape(1, D_MODEL), wg, wu, wd)


def _rope(r, cos_t, sin_t, lane):
    half = QK_ROPE_DIM // 2
    rot = jnp.where(lane < half, -pltpu.roll(r, LANES - half, 1), pltpu.roll(r, half, 1))
    return r * cos_t + rot * sin_t


def _mixer_front_kernel(x_ref, mixg_ref, wu_ref, wv_ref, wcq_ref, wckv_ref, wkr_ref,
                        vng_ref, ws_ref, bs_ref, og_ref,
                        qng_ref, wqup_ref, kvng_ref, wkn_ref, wvv_ref, qhg_ref, khg_ref,
                        cos_ref, sin_ref,
                        ya_ref, q_ref, k_ref, vt_ref):
    tm = x_ref.shape[0]
    x = x_ref[...]
    h = (x * _rms_scale(x, D_MODEL) * mixg_ref[...]).astype(BF16)

    cq = jnp.dot(h, wcq_ref[...], preferred_element_type=F32)
    ckv = jnp.dot(h, wckv_ref[...], preferred_element_type=F32)
    kr = jnp.dot(h, wkr_ref[...], preferred_element_type=F32)
    zv = jnp.dot(h, wv_ref[...], preferred_element_type=F32)
    cqn = (cq * _rms_scale(cq, Q_LORA_RANK) * qng_ref[...]).astype(BF16)
    ckvn = (ckv * _rms_scale(ckv, KV_LORA_RANK) * kvng_ref[...]).astype(BF16)
    qf = jnp.dot(cqn, wqup_ref[...], preferred_element_type=F32)
    kn_all = jnp.dot(ckvn, wkn_ref[...], preferred_element_type=F32)
    vt_ref[...] = jnp.dot(ckvn, wvv_ref[...], preferred_element_type=F32).T.astype(BF16)
    zu = jnp.dot(h, wu_ref[...], preferred_element_type=F32)

    sqrt_half = np.float32(np.sqrt(0.5))

    def gelu(z):
        return 0.5 * z * (1.0 + lax.erf(z * sqrt_half))

    zu = gelu(zu)
    zv = gelu(zv)
    vn = (zv * _rms_scale(zv, GMLP_WIDTH) * vng_ref[...]).astype(BF16)

    cos_t = cos_ref[...].T
    sin_t = sin_ref[...].T
    lane = lax.broadcasted_iota(jnp.int32, (tm, LANES), 1)
    scale = np.float32(QK_HEAD_DIM ** -0.5 * LOG2E)
    qhg_n = qhg_ref[:, :LANES]
    qhg_r = qhg_ref[:, LANES:]
    for hd in range(MLA_HEADS):
        base = hd * QK_PAD_DIM
        qn = qf[:, base:base + LANES]
        qr = qf[:, base + LANES:base + QK_PAD_DIM]
        ss = jnp.sum(qn * qn + qr * qr, axis=-1, keepdims=True)
        inv = lax.rsqrt(ss * (1.0 / QK_HEAD_DIM) + EPS) * scale
        q_ref[:, base:base + LANES] = (qn * inv * qhg_n).astype(BF16)
        q_ref[:, base + LANES:base + QK_PAD_DIM] = _rope(
            qr * inv * qhg_r, cos_t, sin_t, lane).astype(BF16)

    kr_ss = jnp.sum(kr * kr, axis=-1, keepdims=True)
    kr_rot = _rope(kr * khg_ref[:, LANES:], cos_t, sin_t, lane)
    khg_n = khg_ref[:, :LANES]
    for hd in range(MLA_HEADS):
        base = hd * QK_PAD_DIM
        kn = kn_all[:, hd * LANES:(hd + 1) * LANES]
        ss = jnp.sum(kn * kn, axis=-1, keepdims=True) + kr_ss
        inv = lax.rsqrt(ss * (1.0 / QK_HEAD_DIM) + EPS)
        k_ref[:, base:base + LANES] = (kn * inv * khg_n).astype(BF16)
        k_ref[:, base + LANES:base + QK_PAD_DIM] = (kr_rot * inv).astype(BF16)

    t_idx = lax.broadcasted_iota(jnp.int32, (GMLP_CHUNK, GMLP_CHUNK), 0)
    s_idx = lax.broadcasted_iota(jnp.int32, (GMLP_CHUNK, GMLP_CHUNK), 1)
    causal = t_idx >= s_idx
    for g in range(GMLP_GROUPS):
        cols = slice(g * GMLP_GROUP_DIM, (g + 1) * GMLP_GROUP_DIM)
        w_causal = jnp.where(causal, ws_ref[g], 0.0).astype(BF16)
        for c in range(tm // GMLP_CHUNK):
            rows = slice(c * GMLP_CHUNK, (c + 1) * GMLP_CHUNK)
            mixed = jnp.dot(w_causal, vn[rows, cols], preferred_element_type=F32) + bs_ref[g]
            y = zu[rows, cols] * mixed
            y = y * _rms_scale(y, GMLP_GROUP_DIM) * og_ref[:, cols]
            ya_ref[rows, cols] = y.astype(BF16)


def _mixer_front(x, cos_tab, sin_tab, mix_norm_g, w_in, v_norm_g, w_s, b_s, out_g,
                 q_norm_g, w_q_up, kv_norm_g, w_kv_up, q_head_g, k_head_g):
    n_tok = x.shape[0]
    tm = _row_tile(n_tok, MIXER_ROW_TILE)
    H = MLA_HEADS
    c0 = GMLP_WIDTH
    c1 = 2 * GMLP_WIDTH
    c2 = c1 + Q_LORA_RANK
    c3 = c2 + KV_LORA_RANK
    wu = w_in[:, :c0].astype(BF16)
    wv = w_in[:, c0:c1].astype(BF16)
    wcq = w_in[:, c1:c2].astype(BF16)
    wckv = w_in[:, c2:c3].astype(BF16)
    wkr = jnp.pad(w_in[:, c3:], ((0, 0), (0, LANES - QK_ROPE_DIM))).astype(BF16)
    zpad = QK_PAD_DIM - QK_HEAD_DIM
    wqup = jnp.pad(w_q_up.reshape(Q_LORA_RANK, H, QK_HEAD_DIM), ((0, 0), (0, 0), (0, zpad)))
    wqup = wqup.reshape(Q_LORA_RANK, H * QK_PAD_DIM).astype(BF16)
    wkv = w_kv_up.reshape(KV_LORA_RANK, H, QK_NOPE_DIM + V_HEAD_DIM)
    wkn = wkv[:, :, :QK_NOPE_DIM].reshape(KV_LORA_RANK, H * QK_NOPE_DIM).astype(BF16)
    wvv = wkv[:, :, QK_NOPE_DIM:].reshape(KV_LORA_RANK, H * V_HEAD_DIM).astype(BF16)
    qhg = jnp.pad(q_head_g, (0, zpad)).reshape(1, QK_PAD_DIM)
    khg = jnp.pad(k_head_g, (0, zpad)).reshape(1, QK_PAD_DIM)
    bs_full = jnp.broadcast_to(b_s[:, :, None], (GMLP_GROUPS, GMLP_CHUNK, GMLP_GROUP_DIM))

    row = lambda w: pl.BlockSpec((tm, w), lambda i: (i, 0))
    col = lambda h: pl.BlockSpec((h, tm), lambda i: (0, i))
    consts = [mix_norm_g.reshape(1, D_MODEL), wu, wv, wcq, wckv, wkr,
              v_norm_g.reshape(1, GMLP_WIDTH), w_s, bs_full, out_g.reshape(1, GMLP_WIDTH),
              q_norm_g.reshape(1, Q_LORA_RANK), wqup, kv_norm_g.reshape(1, KV_LORA_RANK),
              wkn, wvv, qhg, khg]
    return pl.pallas_call(
        _mixer_front_kernel,
        out_shape=(jax.ShapeDtypeStruct((n_tok, GMLP_WIDTH), BF16),
                   jax.ShapeDtypeStruct((n_tok, H * QK_PAD_DIM), BF16),
                   jax.ShapeDtypeStruct((n_tok, H * QK_PAD_DIM), BF16),
                   jax.ShapeDtypeStruct((H * V_HEAD_DIM, n_tok), BF16)),
        grid=(n_tok // tm,),
        in_specs=[row(D_MODEL)] + [_const_spec(a.shape) for a in consts] + [col(LANES), col(LANES)],
        out_specs=(row(GMLP_WIDTH), row(H * QK_PAD_DIM), row(H * QK_PAD_DIM), col(H * V_HEAD_DIM)),
        compiler_params=pltpu.CompilerParams(
            dimension_semantics=("parallel",), vmem_limit_bytes=VMEM_LIMIT_BYTES),
        name="mixer_front",
    )(x, *consts, cos_tab, sin_tab)


def _attn_kernel(q_ref, k_ref, vt_ref, g_ref, o_ref):
    seq = q_ref.shape[0]
    tq = min(ATTN_Q_TILE, seq)
    nt = (((1,), (1,)), ((), ()))
    key_i = lax.broadcasted_iota(jnp.int32, (tq, tq), 0)
    qry_i = lax.broadcasted_iota(jnp.int32, (tq, tq), 1)
    causal = key_i <= qry_i
    g = g_ref[...]
    n_tiles = seq // tq

    def scores(i):
        lo, hi = i * tq, (i + 1) * tq
        q = q_ref[lo:hi, :]
        s_d = lax.dot_general(k_ref[lo:hi, :], q, nt, preferred_element_type=F32)
        s_d = jnp.where(causal, s_d, F32(-1e30))
        s_o = lax.dot_general(k_ref[0:lo, :], q, nt, preferred_element_type=F32) if i > 0 else None
        return s_d, s_o

    def finish(i, s_d, s_o):
        lo, hi = i * tq, (i + 1) * tq
        m = jnp.max(s_d, axis=0, keepdims=True)
        if i > 0:
            m = jnp.maximum(m, jnp.max(s_o, axis=0, keepdims=True))
        p_d = jnp.exp2(s_d - m)
        l = jnp.sum(p_d, axis=0, keepdims=True)
        o_t = jnp.dot(vt_ref[:, lo:hi], p_d.astype(BF16), preferred_element_type=F32)
        if i > 0:
            p_o = jnp.exp2(s_o - m)
            l = l + jnp.sum(p_o, axis=0, keepdims=True)
            o_t = o_t + jnp.dot(vt_ref[:, 0:lo], p_o.astype(BF16), preferred_element_type=F32)
        o_t = o_t / l
        ms = jnp.sum(o_t * o_t, axis=0, keepdims=True) * (1.0 / V_HEAD_DIM)
        o_ref[lo:hi, :] = ((o_t * lax.rsqrt(ms + EPS)).T * g).astype(BF16)

    order = list(range(n_tiles - 1, -1, -1))
    ahead = [scores(i) for i in order[:ATTN_SCORES_AHEAD]]
    for n, i in enumerate(order):
        if n + ATTN_SCORES_AHEAD < n_tiles:
            ahead.append(scores(order[n + ATTN_SCORES_AHEAD]))
        finish(i, *ahead.pop(0))


def _attention(q, k, vt, out_g, batch, seq):
    H = MLA_HEADS
    return pl.pallas_call(
        _attn_kernel,
        out_shape=jax.ShapeDtypeStruct((batch * seq, H * V_HEAD_DIM), BF16),
        grid=(batch, H),
        in_specs=[pl.BlockSpec((seq, QK_PAD_DIM), lambda b, h: (b, h)),
                  pl.BlockSpec((seq, QK_PAD_DIM), lambda b, h: (b, h)),
                  pl.BlockSpec((V_HEAD_DIM, seq), lambda b, h: (h, b)),
                  pl.BlockSpec((1, V_HEAD_DIM), lambda b, h: (0, h))],
        out_specs=pl.BlockSpec((seq, V_HEAD_DIM), lambda b, h: (b, h)),
        compiler_params=pltpu.CompilerParams(
            dimension_semantics=("parallel", "parallel"),
            vmem_limit_bytes=VMEM_LIMIT_BYTES),
        name="attention",
    )(q, k, vt, out_g.reshape(1, H * V_HEAD_DIM))


def _out_proj_kernel(x_ref, ya_ref, yb_ref, wa_ref, wb_ref, o_ref):
    o_ref[...] = (x_ref[...]
                  + jnp.dot(ya_ref[...], wa_ref[...], preferred_element_type=F32)
                  + jnp.dot(yb_ref[...], wb_ref[...], preferred_element_type=F32))


def _out_proj(x, ya, yb, w_out):
    n_tok = x.shape[0]
    tm = _row_tile(n_tok, OUT_PROJ_ROW_TILE)
    wa = w_out[:GMLP_WIDTH].astype(BF16)
    wb = w_out[GMLP_WIDTH:].astype(BF16)
    row = lambda w: pl.BlockSpec((tm, w), lambda i: (i, 0))
    return pl.pallas_call(
        _out_proj_kernel,
        out_shape=jax.ShapeDtypeStruct((n_tok, D_MODEL), F32),
        grid=(n_tok // tm,),
        in_specs=[row(D_MODEL), row(GMLP_WIDTH), row(MLA_HEADS * V_HEAD_DIM),
                  _const_spec(wa.shape), _const_spec(wb.shape)],
        out_specs=row(D_MODEL),
        compiler_params=pltpu.CompilerParams(
            dimension_semantics=("parallel",), vmem_limit_bytes=VMEM_LIMIT_BYTES),
        name="out_proj",
    )(x, ya, yb, wa, wb)


def kernel(x, positions, ffn1_norm_g, ffn1_w_gate, ffn1_w_up, ffn1_w_down, mix_norm_g, w_in, gmlp_v_norm_g, gmlp_w_s, gmlp_b_s, mla_q_norm_g, mla_w_q_up, mla_kv_norm_g, mla_w_kv_up, mla_q_head_g, mla_k_head_g, gmlp_out_g, mla_out_g, w_out, ffn2_norm_g, ffn2_w_gate, ffn2_w_up, ffn2_w_down):
    B, S, D = x.shape
    assert D == D_MODEL and S % GMLP_CHUNK == 0
    depth = ffn1_norm_g.shape[0]
    cos_tab, sin_tab = _rope_tables(positions)
    xf = x.reshape(B * S, D)
    for l in range(depth):
        xf = _ffn(xf, ffn1_norm_g[l], ffn1_w_gate[l], ffn1_w_up[l], ffn1_w_down[l])
        ya, q, k, vt = _mixer_front(
            xf, cos_tab, sin_tab, mix_norm_g[l], w_in[l], gmlp_v_norm_g[l], gmlp_w_s[l],
            gmlp_b_s[l], gmlp_out_g[l], mla_q_norm_g[l], mla_w_q_up[l], mla_kv_norm_g[l],
            mla_w_kv_up[l], mla_q_head_g[l], mla_k_head_g[l])
        yb = _attention(q, k, vt, mla_out_g[l], B, S)
        xf = _out_proj(xf, ya, yb, w_out[l])
        xf = _ffn(xf, ffn2_norm_g[l], ffn2_w_gate[l], ffn2_w_up[l], ffn2_w_down[l])
    return xf.reshape(B, S, D)
```

```python
import functools

import jax
import jax.numpy as jnp
import numpy as np
from jax import lax
from jax.experimental import pallas as pl
from jax.experimental.pallas import tpu as pltpu

D_MODEL = 2048
GMLP_WIDTH = 1024
GMLP_GROUP_DIM = 128
GMLP_GROUPS = 8
GMLP_CHUNK = 128
V_HEAD_DIM = 128
MLA_HEADS = 8
QK_NOPE_DIM = 128
QK_ROPE_DIM = 64
QK_HEAD_DIM = 192
Q_LORA_RANK = 512
KV_LORA_RANK = 512
ROPE_THETA = 10000.0
D_FF = 5504
EPS = 1e-6

LANES = 128
QK_PAD_DIM = 2 * LANES
FF_TILE = 768
FF_MAIN_STEPS = D_FF // FF_TILE
FF_TAIL = D_FF - FF_MAIN_STEPS * FF_TILE
assert FF_TAIL == LANES
FFN_ROW_TILE = 1024
MIXER_ROW_TILE = 256
OUT_PROJ_ROW_TILE = 512
ATTN_Q_TILE = 256
ATTN_SCORES_AHEAD = 2
VMEM_LIMIT_BYTES = 60 * 1024 * 1024
LOG2E = 1.4426950408889634

F32 = jnp.float32
BF16 = jnp.bfloat16


def _row_tile(n_rows, want):
    t = min(want, n_rows)
    assert n_rows % t == 0, (n_rows, t)
    return t


def _const_spec(shape):
    nd = len(shape)
    return pl.BlockSpec(shape, lambda *_: (0,) * nd, pipeline_mode=pl.Buffered(1))


def _rms_scale(x, width):
    ss = jnp.sum(x * x, axis=-1, keepdims=True)
    return lax.rsqrt(ss * (1.0 / width) + EPS)


def _rope_table_kernel(pos_ref, invf_ref, cos_ref, sin_ref):
    half = QK_ROPE_DIM // 2
    ang = invf_ref[...] * pos_ref[...].astype(F32)
    c = jnp.cos(ang)
    s = jnp.sin(ang)
    zeros = jnp.zeros((LANES - 2 * half, ang.shape[1]), F32)
    for ref, t in ((cos_ref, c), (sin_ref, s)):
        ref[0:half, :] = t
        ref[half:2 * half, :] = t
        ref[2 * half:, :] = zeros


def _rope_tables(positions):
    half = QK_ROPE_DIM // 2
    n_tok = positions.size
    inv_freq = 1.0 / (ROPE_THETA ** (jnp.arange(half, dtype=F32) / half))
    tk = _row_tile(n_tok, 2048)
    return pl.pallas_call(
        _rope_table_kernel,
        out_shape=(jax.ShapeDtypeStruct((LANES, n_tok), F32),) * 2,
        grid=(n_tok // tk,),
        in_specs=[pl.BlockSpec((1, tk), lambda i: (0, i)),
                  pl.BlockSpec((half, 1), lambda i: (0, 0))],
        out_specs=(pl.BlockSpec((LANES, tk), lambda i: (0, i)),) * 2,
        name="rope_tables",
    )(positions.reshape(1, n_tok), inv_freq.reshape(half, 1))


def _ffn_kernel(x_hbm, g_ref, wg_ref, wu_ref, wd_ref, wgu_tail_ref, wd_tail_ref,
                o_ref, h_ref, x_buf, x_sem):
    i = pl.program_id(0)
    j = pl.program_id(1)
    tm = o_ref.shape[0]

    def x_copy(tile):
        rows = pl.ds(pl.multiple_of(tile * tm, tm), tm)
        return pltpu.make_async_copy(x_hbm.at[rows, :], x_buf, x_sem)

    @pl.when(j == 0)
    def _():
        @pl.when(i == 0)
        def _():
            x_copy(0).start()

        x_copy(i).wait()
        x = x_buf[...]
        h_ref[...] = (x * _rms_scale(x, D_MODEL) * g_ref[...]).astype(BF16)
        o_ref[...] = x

    @pl.when((j == 1) & (i + 1 < pl.num_programs(0)))
    def _():
        x_copy(i + 1).start()

    def swiglu_half(gate, up):
        return ((gate / (1.0 + jnp.exp(-gate))) * (up * 0.5)).astype(BF16)

    def step(with_tail):
        h = h_ref[...]
        gate = jnp.dot(h, wg_ref[...], preferred_element_type=F32)
        up = jnp.dot(h, wu_ref[...], preferred_element_type=F32)
        if with_tail:
            gu = jnp.dot(h, wgu_tail_ref[...], preferred_element_type=F32)
        o_ref[...] += jnp.dot(swiglu_half(gate, up), wd_ref[...], preferred_element_type=F32)
        if with_tail:
            o_ref[...] += jnp.dot(swiglu_half(gu[:, :FF_TAIL], gu[:, FF_TAIL:]), wd_tail_ref[...],
                                  preferred_element_type=F32)

    @pl.when(j < FF_MAIN_STEPS - 1)
    def _():
        step(False)

    @pl.when(j == FF_MAIN_STEPS - 1)
    def _():
        step(True)


def _ffn(x, norm_g, w_gate, w_up, w_down):
    n_tok = x.shape[0]
    tm = _row_tile(n_tok, FFN_ROW_TILE)
    wg = w_gate.astype(BF16)
    wu = w_up.astype(BF16)
    wd = w_down.astype(BF16)
    main = FF_MAIN_STEPS * FF_TILE
    wgu_tail = jnp.concatenate([wg[:, main:], wu[:, main:]], axis=1)
    wd_tail = wd[main:]
    return pl.pallas_call(
        _ffn_kernel,
        out_shape=jax.ShapeDtypeStruct((n_tok, D_MODEL), F32),
        grid=(n_tok // tm, FF_MAIN_STEPS),
        in_specs=[pl.BlockSpec(memory_space=pl.ANY),
                  pl.BlockSpec((1, D_MODEL), lambda i, j: (0, 0)),
                  pl.BlockSpec((D_MODEL, FF_TILE), lambda i, j: (0, j)),
                  pl.BlockSpec((D_MODEL, FF_TILE), lambda i, j: (0, j)),
                  pl.BlockSpec((FF_TILE, D_MODEL), lambda i, j: (j, 0)),
                  _const_spec(wgu_tail.shape), _const_spec(wd_tail.shape)],
        out_specs=pl.BlockSpec((tm, D_MODEL), lambda i, j: (i, 0)),
        scratch_shapes=[pltpu.VMEM((tm, D_MODEL), BF16),
                        pltpu.VMEM((tm, D_MODEL), F32),
                        pltpu.SemaphoreType.DMA(())],
        compiler_params=pltpu.CompilerParams(
            dimension_semantics=("arbitrary", "arbitrary"),
            vmem_limit_bytes=VMEM_LIMIT_BYTES),
        name="ffn",
    )(x, norm_g.reshape(1, D_MODEL), wg, wu, wd, wgu_tail, wd_tail)


def _rope(r, cos_t, sin_t, lane):
    half = QK_ROPE_DIM // 2
    rot = jnp.where(lane < half, -pltpu.roll(r, LANES - half, 1), pltpu.roll(r, half, 1))
    return r * cos_t + rot * sin_t


def _mixer_front_kernel(x_ref, mixg_ref, wu_ref, wv_ref, wcq_ref, wckv_ref, wkr_ref,
                        vng_ref, ws_ref, bs_ref, og_ref,
                        qng_ref, wqup_ref, kvng_ref, wkn_ref, wvv_ref, qhg_ref, khg_ref,
                        cos_ref, sin_ref,
                        ya_ref, q_ref, k_ref, vt_ref):
    tm = x_ref.shape[0]
    x = x_ref[...]
    h = (x * _rms_scale(x, D_MODEL) * mixg_ref[...]).astype(BF16)

    cq = jnp.dot(h, wcq_ref[...], preferred_element_type=F32)
    ckv = jnp.dot(h, wckv_ref[...], preferred_element_type=F32)
    kr = jnp.dot(h, wkr_ref[...], preferred_element_type=F32)
    zv = jnp.dot(h, wv_ref[...], preferred_element_type=F32)
    cqn = (cq * _rms_scale(cq, Q_LORA_RANK) * qng_ref[...]).astype(BF16)
    ckvn = (ckv * _rms_scale(ckv, KV_LORA_RANK) * kvng_ref[...]).astype(BF16)
    qf = jnp.dot(cqn, wqup_ref[...], preferred_element_type=F32)
    kn_all = jnp.dot(ckvn, wkn_ref[...], preferred_element_type=F32)
    vt_ref[...] = jnp.dot(ckvn, wvv_ref[...], preferred_element_type=F32).T.astype(BF16)
    zu = jnp.dot(h, wu_ref[...], preferred_element_type=F32)

    sqrt_half = np.float32(np.sqrt(0.5))

    def gelu(z):
        return 0.5 * z * (1.0 + lax.erf(z * sqrt_half))

    zu = gelu(zu)
    zv = gelu(zv)
    vn = (zv * _rms_scale(zv, GMLP_WIDTH) * vng_ref[...]).astype(BF16)

    cos_t = cos_ref[...].T
    sin_t = sin_ref[...].T
    lane = lax.broadcasted_iota(jnp.int32, (tm, LANES), 1)
    scale = np.float32(QK_HEAD_DIM ** -0.5 * LOG2E)
    qhg_n = qhg_ref[:, :LANES]
    qhg_r = qhg_ref[:, LANES:]
    for hd in range(MLA_HEADS):
        base = hd * QK_PAD_DIM
        qn = qf[:, base:base + LANES]
        qr = qf[:, base + LANES:base + QK_PAD_DIM]
        ss = jnp.sum(qn * qn + qr * qr, axis=-1, keepdims=True)
        inv = lax.rsqrt(ss * (1.0 / QK_HEAD_DIM) + EPS) * scale
        q_ref[:, base:base + LANES] = (qn * inv * qhg_n).astype(BF16)
        q_ref[:, base + LANES:base + QK_PAD_DIM] = _rope(
            qr * inv * qhg_r, cos_t, sin_t, lane).astype(BF16)

    kr_ss = jnp.sum(kr * kr, axis=-1, keepdims=True)
    kr_rot = _rope(kr * khg_ref[:, LANES:], cos_t, sin_t, lane)
    khg_n = khg_ref[:, :LANES]
    for hd in range(MLA_HEADS):
        base = hd * QK_PAD_DIM
        kn = kn_all[:, hd * LANES:(hd + 1) * LANES]
        ss = jnp.sum(kn * kn, axis=-1, keepdims=True) + kr_ss
        inv = lax.rsqrt(ss * (1.0 / QK_HEAD_DIM) + EPS)
        k_ref[:, base:base + LANES] = (kn * inv * khg_n).astype(BF16)
        k_ref[:, base + LANES:base + QK_PAD_DIM] = (kr_rot * inv).astype(BF16)

    t_idx = lax.broadcasted_iota(jnp.int32, (GMLP_CHUNK, GMLP_CHUNK), 0)
    s_idx = lax.broadcasted_iota(jnp.int32, (GMLP_CHUNK, GMLP_CHUNK), 1)
    causal = t_idx >= s_idx
    for g in range(GMLP_GROUPS):
        cols = slice(g * GMLP_GROUP_DIM, (g + 1) * GMLP_GROUP_DIM)
        w_causal = jnp.where(causal, ws_ref[g], 0.0).astype(BF16)
        for c in range(tm // GMLP_CHUNK):
            rows = slice(c * GMLP_CHUNK, (c + 1) * GMLP_CHUNK)
            mixed = jnp.dot(w_causal, vn[rows, cols], preferred_element_type=F32) + bs_ref[g]
            y = zu[rows, cols] * mixed
            y = y * _rms_scale(y, GMLP_GROUP_DIM) * og_ref[:, cols]
            ya_ref[rows, cols] = y.astype(BF16)


def _mixer_front(x, cos_tab, sin_tab, mix_norm_g, w_in, v_norm_g, w_s, b_s, out_g,
                 q_norm_g, w_q_up, kv_norm_g, w_kv_up, q_head_g, k_head_g):
    n_tok = x.shape[0]
    tm = _row_tile(n_tok, MIXER_ROW_TILE)
    H = MLA_HEADS
    c0 = GMLP_WIDTH
    c1 = 2 * GMLP_WIDTH
    c2 = c1 + Q_LORA_RANK
    c3 = c2 + KV_LORA_RANK
    wu = w_in[:, :c0].astype(BF16)
    wv = w_in[:, c0:c1].astype(BF16)
    wcq = w_in[:, c1:c2].astype(BF16)
    wckv = w_in[:, c2:c3].astype(BF16)
    wkr = jnp.pad(w_in[:, c3:], ((0, 0), (0, LANES - QK_ROPE_DIM))).astype(BF16)
    zpad = QK_PAD_DIM - QK_HEAD_DIM
    wqup = jnp.pad(w_q_up.reshape(Q_LORA_RANK, H, QK_HEAD_DIM), ((0, 0), (0, 0), (0, zpad)))
    wqup = wqup.reshape(Q_LORA_RANK, H * QK_PAD_DIM).astype(BF16)
    wkv = w_kv_up.reshape(KV_LORA_RANK, H, QK_NOPE_DIM + V_HEAD_DIM)
    wkn = wkv[:, :, :QK_NOPE_DIM].reshape(KV_LORA_RANK, H * QK_NOPE_DIM).astype(BF16)
    wvv = wkv[:, :, QK_NOPE_DIM:].reshape(KV_LORA_RANK, H * V_HEAD_DIM).astype(BF16)
    qhg = jnp.pad(q_head_g, (0, zpad)).reshape(1, QK_PAD_DIM)
    khg = jnp.pad(k_head_g, (0, zpad)).reshape(1, QK_PAD_DIM)
    bs_full = jnp.broadcast_to(b_s[:, :, None], (GMLP_GROUPS, GMLP_CHUNK, GMLP_GROUP_DIM))

    row = lambda w: pl.BlockSpec((tm, w), lambda i: (i, 0))
    col = lambda h: pl.BlockSpec((h, tm), lambda i: (0, i))
    consts = [mix_norm_g.reshape(1, D_MODEL), wu, wv, wcq, wckv, wkr,
              v_norm_g.reshape(1, GMLP_WIDTH), w_s, bs_full, out_g.reshape(1, GMLP_WIDTH),
              q_norm_g.reshape(1, Q_LORA_RANK), wqup, kv_norm_g.reshape(1, KV_LORA_RANK),
              wkn, wvv, qhg, khg]
    return pl.pallas_call(
        _mixer_front_kernel,
        out_shape=(jax.ShapeDtypeStruct((n_tok, GMLP_WIDTH), BF16),
                   jax.ShapeDtypeStruct((n_tok, H * QK_PAD_DIM), BF16),
                   jax.ShapeDtypeStruct((n_tok, H * QK_PAD_DIM), BF16),
                   jax.ShapeDtypeStruct((H * V_HEAD_DIM, n_tok), BF16)),
        grid=(n_tok // tm,),
        in_specs=[row(D_MODEL)] + [_const_spec(a.shape) for a in consts] + [col(LANES), col(LANES)],
        out_specs=(row(GMLP_WIDTH), row(H * QK_PAD_DIM), row(H * QK_PAD_DIM), col(H * V_HEAD_DIM)),
        compiler_params=pltpu.CompilerParams(
            dimension_semantics=("parallel",), vmem_limit_bytes=VMEM_LIMIT_BYTES),
        name="mixer_front",
    )(x, *consts, cos_tab, sin_tab)


def _attn_kernel(q_ref, k_ref, vt_ref, g_ref, o_ref):
    seq = q_ref.shape[0]
    tq = min(ATTN_Q_TILE, seq)
    nt = (((1,), (1,)), ((), ()))
    key_i = lax.broadcasted_iota(jnp.int32, (tq, tq), 0)
    qry_i = lax.broadcasted_iota(jnp.int32, (tq, tq), 1)
    causal = key_i <= qry_i
    g = g_ref[...]
    n_tiles = seq // tq

    def scores(i):
        lo, hi = i * tq, (i + 1) * tq
        q = q_ref[lo:hi, :]
        s_d = lax.dot_general(k_ref[lo:hi, :], q, nt, preferred_element_type=F32)
        s_d = jnp.where(causal, s_d, F32(-1e30))
        s_o = lax.dot_general(k_ref[0:lo, :], q, nt, preferred_element_type=F32) if i > 0 else None
        return s_d, s_o

    def finish(i, s_d, s_o):
        lo, hi = i * tq, (i + 1) * tq
        m = jnp.max(s_d, axis=0, keepdims=True)
        if i > 0:
            m = jnp.maximum(m, jnp.max(s_o, axis=0, keepdims=True))
        p_d = jnp.exp2(s_d - m)
        l = jnp.sum(p_d, axis=0, keepdims=True)
        o_t = jnp.dot(vt_ref[:, lo:hi], p_d.astype(BF16), preferred_element_type=F32)
        if i > 0:
            p_o = jnp.exp2(s_o - m)
            l = l + jnp.sum(p_o, axis=0, keepdims=True)
            o_t = o_t + jnp.dot(vt_ref[:, 0:lo], p_o.astype(BF16), preferred_element_type=F32)
        o_t = o_t / l
        ms = jnp.sum(o_t * o_t, axis=0, keepdims=True) * (1.0 / V_HEAD_DIM)
        o_ref[lo:hi, :] = ((o_t * lax.rsqrt(ms + EPS)).T * g).astype(BF16)

    order = list(range(n_tiles - 1, -1, -1))
    ahead = [scores(i) for i in order[:ATTN_SCORES_AHEAD]]
    for n, i in enumerate(order):
        if n + ATTN_SCORES_AHEAD < n_tiles:
            ahead.append(scores(order[n + ATTN_SCORES_AHEAD]))
        finish(i, *ahead.pop(0))


def _attention(q, k, vt, out_g, batch, seq):
    H = MLA_HEADS
    return pl.pallas_call(
        _attn_kernel,
        out_shape=jax.ShapeDtypeStruct((batch * seq, H * V_HEAD_DIM), BF16),
        grid=(batch, H),
        in_specs=[pl.BlockSpec((seq, QK_PAD_DIM), lambda b, h: (b, h)),
                  pl.BlockSpec((seq, QK_PAD_DIM), lambda b, h: (b, h)),
                  pl.BlockSpec((V_HEAD_DIM, seq), lambda b, h: (h, b)),
                  pl.BlockSpec((1, V_HEAD_DIM), lambda b, h: (0, h))],
        out_specs=pl.BlockSpec((seq, V_HEAD_DIM), lambda b, h: (b, h)),
        compiler_params=pltpu.CompilerParams(
            dimension_semantics=("parallel", "parallel"),
            vmem_limit_bytes=VMEM_LIMIT_BYTES),
        name="attention",
    )(q, k, vt, out_g.reshape(1, H * V_HEAD_DIM))


def _out_proj_kernel(x_ref, ya_ref, yb_ref, wa_ref, wb_ref, o_ref):
    o_ref[...] = (x_ref[...]
                  + jnp.dot(ya_ref[...], wa_ref[...], preferred_element_type=F32)
                  + jnp.dot(yb_ref[...], wb_ref[...], preferred_element_type=F32))


def _out_proj(x, ya, yb, w_out):
    n_tok = x.shape[0]
    tm = _row_tile(n_tok, OUT_PROJ_ROW_TILE)
    wa = w_out[:GMLP_WIDTH].astype(BF16)
    wb = w_out[GMLP_WIDTH:].astype(BF16)
    row = lambda w: pl.BlockSpec((tm, w), lambda i: (i, 0))
    return pl.pallas_call(
        _out_proj_kernel,
        out_shape=jax.ShapeDtypeStruct((n_tok, D_MODEL), F32),
        grid=(n_tok // tm,),
        in_specs=[row(D_MODEL), row(GMLP_WIDTH), row(MLA_HEADS * V_HEAD_DIM),
                  _const_spec(wa.shape), _const_spec(wb.shape)],
        out_specs=row(D_MODEL),
        compiler_params=pltpu.CompilerParams(
            dimension_semantics=("parallel",), vmem_limit_bytes=VMEM_LIMIT_BYTES),
        name="out_proj",
    )(x, ya, yb, wa, wb)


def kernel(x, positions, ffn1_norm_g, ffn1_w_gate, ffn1_w_up, ffn1_w_down, mix_norm_g, w_in, gmlp_v_norm_g, gmlp_w_s, gmlp_b_s, mla_q_norm_g, mla_w_q_up, mla_kv_norm_g, mla_w_kv_up, mla_q_head_g, mla_k_head_g, gmlp_out_g, mla_out_g, w_out, ffn2_norm_g, ffn2_w_gate, ffn2_w_up, ffn2_w_down):
    B, S, D = x.shape
    assert D == D_MODEL and S % GMLP_CHUNK == 0
    depth = ffn1_norm_g.shape[0]
    cos_tab, sin_tab = _rope_tables(positions)
    xf = x.reshape(B * S, D)
    for l in range(depth):
        xf = _ffn(xf, ffn1_norm_g[l], ffn1_w_gate[l], ffn1_w_up[l], ffn1_w_down[l])
        ya, q, k, vt = _mixer_front(
            xf, cos_tab, sin_tab, mix_norm_g[l], w_in[l], gmlp_v_norm_g[l], gmlp_w_s[l],
            gmlp_b_s[l], gmlp_out_g[l], mla_q_norm_g[l], mla_w_q_up[l], mla_kv_norm_g[l],
            mla_w_kv_up[l], mla_q_head_g[l], mla_k_head_g[l])
        yb = _attention(q, k, vt, mla_out_g[l], B, S)
        xf = _out_proj(xf, ya, yb, w_out[l])
        xf = _ffn(xf, ffn2_norm_g[l], ffn2_w_gate[l], ffn2_w_up[l], ffn2_w_down[l])
    return xf.reshape(B, S, D)
```

```python
import functools

import jax
import jax.numpy as jnp
import numpy as np
from jax import lax
from jax.experimental import pallas as pl
from jax.experimental.pallas import tpu as pltpu

D_MODEL = 2048
GMLP_WIDTH = 1024
GMLP_GROUP_DIM = 128
GMLP_GROUPS = 8
GMLP_CHUNK = 128
V_HEAD_DIM = 128
MLA_HEADS = 8
QK_NOPE_DIM = 128
QK_ROPE_DIM = 64
QK_HEAD_DIM = 192
Q_LORA_RANK = 512
KV_LORA_RANK = 512
ROPE_THETA = 10000.0
D_FF = 5504
EPS = 1e-6

LANES = 128
BF16_SUBLANES = 16
QK_PAD_DIM = 2 * LANES
FF_TILE = 768
FF_MAIN_STEPS = D_FF // FF_TILE
FF_TAIL = D_FF - FF_MAIN_STEPS * FF_TILE
assert FF_TAIL == LANES
FFN_ROW_TILE = 1024
MIXER_ROW_TILE = 256
OUT_PROJ_ROW_TILE = 512
ATTN_Q_TILE = 256
ATTN_SCORES_AHEAD = 3
VMEM_LIMIT_BYTES = 60 * 1024 * 1024
LOG2E = 1.4426950408889634

F32 = jnp.float32
BF16 = jnp.bfloat16


def _row_tile(n_rows, want):
    t = min(want, n_rows)
    assert n_rows % t == 0, (n_rows, t)
    return t


def _const_spec(shape):
    nd = len(shape)
    return pl.BlockSpec(shape, lambda *_: (0,) * nd, pipeline_mode=pl.Buffered(1))


def _rms_scale(x, width):
    ss = jnp.sum(x * x, axis=-1, keepdims=True)
    return lax.rsqrt(ss * (1.0 / width) + EPS)


def _rope_table_kernel(pos_ref, invf_ref, cos_ref, sin_ref):
    half = QK_ROPE_DIM // 2
    ang = invf_ref[...] * pos_ref[...].astype(F32)
    c = jnp.cos(ang)
    s = jnp.sin(ang)
    zeros = jnp.zeros((LANES - 2 * half, ang.shape[1]), F32)
    for ref, t in ((cos_ref, c), (sin_ref, s)):
        ref[0:half, :] = t
        ref[half:2 * half, :] = t
        ref[2 * half:, :] = zeros


def _rope_tables(positions):
    half = QK_ROPE_DIM // 2
    n_tok = positions.size
    inv_freq = 1.0 / (ROPE_THETA ** (jnp.arange(half, dtype=F32) / half))
    tk = _row_tile(n_tok, 2048)
    return pl.pallas_call(
        _rope_table_kernel,
        out_shape=(jax.ShapeDtypeStruct((LANES, n_tok), F32),) * 2,
        grid=(n_tok // tk,),
        in_specs=[pl.BlockSpec((1, tk), lambda i: (0, i)),
                  pl.BlockSpec((half, 1), lambda i: (0, 0))],
        out_specs=(pl.BlockSpec((LANES, tk), lambda i: (0, i)),) * 2,
        name="rope_tables",
    )(positions.reshape(1, n_tok), inv_freq.reshape(half, 1))


def _ffn_kernel(x_hbm, g_ref, wg_ref, wu_ref, wd_ref, wgu_tail_ref, wd_tail_ref,
                o_ref, h_ref, x_buf, x_sem):
    i = pl.program_id(0)
    j = pl.program_id(1)
    n_tiles = pl.num_programs(0)
    tm = o_ref.shape[0]
    slot = i % 2
    norm_steps = FF_MAIN_STEPS - 2
    norm_rows = -(-tm // (norm_steps * BF16_SUBLANES)) * BF16_SUBLANES

    def x_copy(tile):
        rows = pl.ds(pl.multiple_of(tile * tm, tm), tm)
        return pltpu.make_async_copy(x_hbm.at[rows, :], x_buf, x_sem)

    def normalise(rows, dst_slot):
        x = x_buf[rows, :]
        h_ref[dst_slot, rows, :] = (x * _rms_scale(x, D_MODEL) * g_ref[...]).astype(BF16)

    @pl.when((i == 0) & (j == 0))
    def _():
        x_copy(0).start()
        x_copy(0).wait()
        normalise(slice(None), 0)

    @pl.when((j == 1) & (i + 1 < n_tiles))
    def _():
        x_copy(i + 1).wait()

    def swiglu_half(gate, up):
        return ((gate / (1.0 + jnp.exp(-gate))) * (up * 0.5)).astype(BF16)

    def step(first=False, norm_next=False, with_tail=False):
        h = h_ref[slot]
        gate = jnp.dot(h, wg_ref[...], preferred_element_type=F32)
        up = jnp.dot(h, wu_ref[...], preferred_element_type=F32)
        if with_tail:
            gu = jnp.dot(h, wgu_tail_ref[...], preferred_element_type=F32)
        acc = jnp.dot(swiglu_half(gate, up), wd_ref[...], preferred_element_type=F32)
        if first:
            o_ref[...] = x_buf[...] + acc
        else:
            o_ref[...] += acc
        if with_tail:
            o_ref[...] += jnp.dot(swiglu_half(gu[:, :FF_TAIL], gu[:, FF_TAIL:]), wd_tail_ref[...],
                                  preferred_element_type=F32)
        if norm_next:
            start = jnp.minimum((j - 1) * norm_rows, tm - norm_rows)
            normalise(pl.ds(pl.multiple_of(start, BF16_SUBLANES), norm_rows), 1 - slot)

    @pl.when(j == 0)
    def _():
        step(first=True)

    @pl.when((j == 0) & (i + 1 < n_tiles))
    def _():
        x_copy(i + 1).start()

    @pl.when((j > 0) & (j < FF_MAIN_STEPS - 1))
    def _():
        step(norm_next=True)

    @pl.when(j == FF_MAIN_STEPS - 1)
    def _():
        step(with_tail=True)


def _ffn(x, norm_g, w_gate, w_up, w_down):
    n_tok = x.shape[0]
    tm = _row_tile(n_tok, FFN_ROW_TILE)
    wg = w_gate.astype(BF16)
    wu = w_up.astype(BF16)
    wd = w_down.astype(BF16)
    main = FF_MAIN_STEPS * FF_TILE
    wgu_tail = jnp.concatenate([wg[:, main:], wu[:, main:]], axis=1)
    wd_tail = wd[main:]
    return pl.pallas_call(
        _ffn_kernel,
        out_shape=jax.ShapeDtypeStruct((n_tok, D_MODEL), F32),
        grid=(n_tok // tm, FF_MAIN_STEPS),
        in_specs=[pl.BlockSpec(memory_space=pl.ANY),
                  pl.BlockSpec((1, D_MODEL), lambda i, j: (0, 0)),
                  pl.BlockSpec((D_MODEL, FF_TILE), lambda i, j: (0, j)),
                  pl.BlockSpec((D_MODEL, FF_TILE), lambda i, j: (0, j)),
                  pl.BlockSpec((FF_TILE, D_MODEL), lambda i, j: (j, 0)),
                  _const_spec(wgu_tail.shape), _const_spec(wd_tail.shape)],
        out_specs=pl.BlockSpec((tm, D_MODEL), lambda i, j: (i, 0)),
        scratch_shapes=[pltpu.VMEM((2, tm, D_MODEL), BF16),
                        pltpu.VMEM((tm, D_MODEL), F32),
                        pltpu.SemaphoreType.DMA(())],
        compiler_params=pltpu.CompilerParams(
            dimension_semantics=("arbitrary", "arbitrary"),
            vmem_limit_bytes=VMEM_LIMIT_BYTES),
        name="ffn",
    )(x, norm_g.reshape(1, D_MODEL), wg, wu, wd, wgu_tail, wd_tail)


def _rope(r, cos_t, sin_t, lane):
    half = QK_ROPE_DIM // 2
    rot = jnp.where(lane < half, -pltpu.roll(r, LANES - half, 1), pltpu.roll(r, half, 1))
    return r * cos_t + rot * sin_t


def _mixer_front_kernel(x_ref, mixg_ref, wu_ref, wv_ref, wcq_ref, wckv_ref, wkr_ref,
                        vng_ref, ws_ref, bs_ref, og_ref,
                        qng_ref, wqup_ref, kvng_ref, wkn_ref, wvv_ref, qhg_ref, khg_ref,
                        cos_ref, sin_ref,
                        ya_ref, q_ref, k_ref, vt_ref):
    tm = x_ref.shape[0]
    x = x_ref[...]
    h = (x * _rms_scale(x, D_MODEL) * mixg_ref[...]).astype(BF16)

    cq = jnp.dot(h, wcq_ref[...], preferred_element_type=F32)
    ckv = jnp.dot(h, wckv_ref[...], preferred_element_type=F32)
    kr = jnp.dot(h, wkr_ref[...], preferred_element_type=F32)
    zv = jnp.dot(h, wv_ref[...], preferred_element_type=F32)
    cqn = (cq * _rms_scale(cq, Q_LORA_RANK) * qng_ref[...]).astype(BF16)
    ckvn = (ckv * _rms_scale(ckv, KV_LORA_RANK) * kvng_ref[...]).astype(BF16)
    qf = jnp.dot(cqn, wqup_ref[...], preferred_element_type=F32)
    kn_all = jnp.dot(ckvn, wkn_ref[...], preferred_element_type=F32)
    vt_ref[...] = jnp.dot(ckvn, wvv_ref[...], preferred_element_type=F32).T.astype(BF16)
    zu = jnp.dot(h, wu_ref[...], preferred_element_type=F32)

    sqrt_half = np.float32(np.sqrt(0.5))

    def gelu(z):
        return 0.5 * z * (1.0 + lax.erf(z * sqrt_half))

    zu = gelu(zu)
    zv = gelu(zv)
    vn = (zv * _rms_scale(zv, GMLP_WIDTH) * vng_ref[...]).astype(BF16)

    cos_t = cos_ref[...].T
    sin_t = sin_ref[...].T
    lane = lax.broadcasted_iota(jnp.int32, (tm, LANES), 1)
    scale = np.float32(QK_HEAD_DIM ** -0.5 * LOG2E)
    qhg_n = qhg_ref[:, :LANES]
    qhg_r = qhg_ref[:, LANES:]
    for hd in range(MLA_HEADS):
        base = hd * QK_PAD_DIM
        qn = qf[:, base:base + LANES]
        qr = qf[:, base + LANES:base + QK_PAD_DIM]
        ss = jnp.sum(qn * qn + qr * qr, axis=-1, keepdims=True)
        inv = lax.rsqrt(ss * (1.0 / QK_HEAD_DIM) + EPS) * scale
        q_ref[:, base:base + LANES] = (qn * inv * qhg_n).astype(BF16)
        q_ref[:, base + LANES:base + QK_PAD_DIM] = _rope(
            qr * inv * qhg_r, cos_t, sin_t, lane).astype(BF16)

    kr_ss = jnp.sum(kr * kr, axis=-1, keepdims=True)
    kr_rot = _rope(kr * khg_ref[:, LANES:], cos_t, sin_t, lane)
    khg_n = khg_ref[:, :LANES]
    for hd in range(MLA_HEADS):
        base = hd * QK_PAD_DIM
        kn = kn_all[:, hd * LANES:(hd + 1) * LANES]
        ss = jnp.sum(kn * kn, axis=-1, keepdims=True) + kr_ss
        inv = lax.rsqrt(ss * (1.0 / QK_HEAD_DIM) + EPS)
        k_ref[:, base:base + LANES] = (kn * inv * khg_n).astype(BF16)
        k_ref[:, base + LANES:base + QK_PAD_DIM] = (kr_rot * inv).astype(BF16)

    t_idx = lax.broadcasted_iota(jnp.int32, (GMLP_CHUNK, GMLP_CHUNK), 0)
    s_idx = lax.broadcasted_iota(jnp.int32, (GMLP_CHUNK, GMLP_CHUNK), 1)
    causal = t_idx >= s_idx
    for g in range(GMLP_GROUPS):
        cols = slice(g * GMLP_GROUP_DIM, (g + 1) * GMLP_GROUP_DIM)
        w_causal = jnp.where(causal, ws_ref[g], 0.0).astype(BF16)
        for c in range(tm // GMLP_CHUNK):
            rows = slice(c * GMLP_CHUNK, (c + 1) * GMLP_CHUNK)
            mixed = jnp.dot(w_causal, vn[rows, cols], preferred_element_type=F32) + bs_ref[g]
            y = zu[rows, cols] * mixed
            y = y * _rms_scale(y, GMLP_GROUP_DIM) * og_ref[:, cols]
            ya_ref[rows, cols] = y.astype(BF16)


def _mixer_front(x, cos_tab, sin_tab, mix_norm_g, w_in, v_norm_g, w_s, b_s, out_g,
                 q_norm_g, w_q_up, kv_norm_g, w_kv_up, q_head_g, k_head_g):
    n_tok = x.shape[0]
    tm = _row_tile(n_tok, MIXER_ROW_TILE)
    H = MLA_HEADS
    c0 = GMLP_WIDTH
    c1 = 2 * GMLP_WIDTH
    c2 = c1 + Q_LORA_RANK
    c3 = c2 + KV_LORA_RANK
    wu = w_in[:, :c0].astype(BF16)
    wv = w_in[:, c0:c1].astype(BF16)
    wcq = w_in[:, c1:c2].astype(BF16)
    wckv = w_in[:, c2:c3].astype(BF16)
    wkr = jnp.pad(w_in[:, c3:], ((0, 0), (0, LANES - QK_ROPE_DIM))).astype(BF16)
    zpad = QK_PAD_DIM - QK_HEAD_DIM
    wqup = jnp.pad(w_q_up.reshape(Q_LORA_RANK, H, QK_HEAD_DIM), ((0, 0), (0, 0), (0, zpad)))
    wqup = wqup.reshape(Q_LORA_RANK, H * QK_PAD_DIM).astype(BF16)
    wkv = w_kv_up.reshape(KV_LORA_RANK, H, QK_NOPE_DIM + V_HEAD_DIM)
    wkn = wkv[:, :, :QK_NOPE_DIM].reshape(KV_LORA_RANK, H * QK_NOPE_DIM).astype(BF16)
    wvv = wkv[:, :, QK_NOPE_DIM:].reshape(KV_LORA_RANK, H * V_HEAD_DIM).astype(BF16)
    qhg = jnp.pad(q_head_g, (0, zpad)).reshape(1, QK_PAD_DIM)
    khg = jnp.pad(k_head_g, (0, zpad)).reshape(1, QK_PAD_DIM)
    bs_full = jnp.broadcast_to(b_s[:, :, None], (GMLP_GROUPS, GMLP_CHUNK, GMLP_GROUP_DIM))

    row = lambda w: pl.BlockSpec((tm, w), lambda i: (i, 0))
    col = lambda h: pl.BlockSpec((h, tm), lambda i: (0, i))
    consts = [mix_norm_g.reshape(1, D_MODEL), wu, wv, wcq, wckv, wkr,
              v_norm_g.reshape(1, GMLP_WIDTH), w_s, bs_full, out_g.reshape(1, GMLP_WIDTH),
              q_norm_g.reshape(1, Q_LORA_RANK), wqup, kv_norm_g.reshape(1, KV_LORA_RANK),
              wkn, wvv, qhg, khg]
    return pl.pallas_call(
        _mixer_front_kernel,
        out_shape=(jax.ShapeDtypeStruct((n_tok, GMLP_WIDTH), BF16),
                   jax.ShapeDtypeStruct((n_tok, H * QK_PAD_DIM), BF16),
                   jax.ShapeDtypeStruct((n_tok, H * QK_PAD_DIM), BF16),
                   jax.ShapeDtypeStruct((H * V_HEAD_DIM, n_tok), BF16)),
        grid=(n_tok // tm,),
        in_specs=[row(D_MODEL)] + [_const_spec(a.shape) for a in consts] + [col(LANES), col(LANES)],
        out_specs=(row(GMLP_WIDTH), row(H * QK_PAD_DIM), row(H * QK_PAD_DIM), col(H * V_HEAD_DIM)),
        compiler_params=pltpu.CompilerParams(
            dimension_semantics=("parallel",), vmem_limit_bytes=VMEM_LIMIT_BYTES),
        name="mixer_front",
    )(x, *consts, cos_tab, sin_tab)


def _attn_kernel(q_ref, k_ref, vt_ref, g_ref, o_ref):
    seq = q_ref.shape[0]
    tq = min(ATTN_Q_TILE, seq)
    nt = (((1,), (1,)), ((), ()))
    key_i = lax.broadcasted_iota(jnp.int32, (tq, tq), 0)
    qry_i = lax.broadcasted_iota(jnp.int32, (tq, tq), 1)
    causal = key_i <= qry_i
    g = g_ref[...]
    n_tiles = seq // tq

    def scores(i):
        lo, hi = i * tq, (i + 1) * tq
        q = q_ref[lo:hi, :]
        s_d = lax.dot_general(k_ref[lo:hi, :], q, nt, preferred_element_type=F32)
        s_d = jnp.where(causal, s_d, F32(-1e30))
        s_o = lax.dot_general(k_ref[0:lo, :], q, nt, preferred_element_type=F32) if i > 0 else None
        return s_d, s_o

    def finish(i, s_d, s_o):
        lo, hi = i * tq, (i + 1) * tq
        m = jnp.max(s_d, axis=0, keepdims=True)
        if i > 0:
            m = jnp.maximum(m, jnp.max(s_o, axis=0, keepdims=True))
        p_d = jnp.exp2(s_d - m)
        l = jnp.sum(p_d, axis=0, keepdims=True)
        o_t = jnp.dot(vt_ref[:, lo:hi], p_d.astype(BF16), preferred_element_type=F32)
        if i > 0:
            p_o = jnp.exp2(s_o - m)
            l = l + jnp.sum(p_o, axis=0, keepdims=True)
            o_t = o_t + jnp.dot(vt_ref[:, 0:lo], p_o.astype(BF16), preferred_element_type=F32)
        o_t = o_t / l
        ms = jnp.sum(o_t * o_t, axis=0, keepdims=True) * (1.0 / V_HEAD_DIM)
        o_ref[lo:hi, :] = ((o_t * lax.rsqrt(ms + EPS)).T * g).astype(BF16)

    order = list(range(n_tiles - 1, -1, -1))
    ahead = [scores(i) for i in order[:ATTN_SCORES_AHEAD]]
    for n, i in enumerate(order):
        if n + ATTN_SCORES_AHEAD < n_tiles:
            ahead.append(scores(order[n + ATTN_SCORES_AHEAD]))
        finish(i, *ahead.pop(0))


def _attention(q, k, vt, out_g, batch, seq):
    H = MLA_HEADS
    return pl.pallas_call(
        _attn_kernel,
        out_shape=jax.ShapeDtypeStruct((batch * seq, H * V_HEAD_DIM), BF16),
        grid=(batch, H),
        in_specs=[pl.BlockSpec((seq, QK_PAD_DIM), lambda b, h: (b, h)),
                  pl.BlockSpec((seq, QK_PAD_DIM), lambda b, h: (b, h)),
                  pl.BlockSpec((V_HEAD_DIM, seq), lambda b, h: (h, b)),
                  pl.BlockSpec((1, V_HEAD_DIM), lambda b, h: (0, h))],
        out_specs=pl.BlockSpec((seq, V_HEAD_DIM), lambda b, h: (b, h)),
        compiler_params=pltpu.CompilerParams(
            dimension_semantics=("parallel", "parallel"),
            vmem_limit_bytes=VMEM_LIMIT_BYTES),
        name="attention",
    )(q, k, vt, out_g.reshape(1, H * V_HEAD_DIM))


def _out_proj_kernel(x_ref, ya_ref, yb_ref, wa_ref, wb_ref, *rest):
    n_cast = (len(rest) - 1) // 2
    cast_in, o_ref, cast_out = rest[:n_cast], rest[n_cast], rest[n_cast + 1:]
    o_ref[...] = (x_ref[...]
                  + jnp.dot(ya_ref[...], wa_ref[...], preferred_element_type=F32)
                  + jnp.dot(yb_ref[...], wb_ref[...], preferred_element_type=F32))
    for src, dst in zip(cast_in, cast_out):
        dst[...] = src[...].astype(BF16)


def _cast_slab_spec(n_rows, n_cols, n_steps):
    for unit in (BF16_SUBLANES, LANES):
        rows = min(-(-n_rows // (n_steps * unit)) * unit, n_rows)
        if n_rows % rows == 0:
            break
    assert n_rows % rows == 0, (n_rows, n_steps)
    last = n_rows // rows - 1
    return pl.BlockSpec((rows, n_cols), lambda i: (jnp.minimum(i, last), 0))


def _out_proj(x, ya, yb, w_out, cast_weights):
    n_tok = x.shape[0]
    tm = _row_tile(n_tok, OUT_PROJ_ROW_TILE)
    n_steps = n_tok // tm
    wa = w_out[:GMLP_WIDTH].astype(BF16)
    wb = w_out[GMLP_WIDTH:].astype(BF16)
    row = lambda w: pl.BlockSpec((tm, w), lambda i: (i, 0))
    cast_specs = [_cast_slab_spec(*w.shape, n_steps) for w in cast_weights]
    return pl.pallas_call(
        _out_proj_kernel,
        out_shape=(jax.ShapeDtypeStruct((n_tok, D_MODEL), F32),
                   *[jax.ShapeDtypeStruct(w.shape, BF16) for w in cast_weights]),
        grid=(n_steps,),
        in_specs=[row(D_MODEL), row(GMLP_WIDTH), row(MLA_HEADS * V_HEAD_DIM),
                  _const_spec(wa.shape), _const_spec(wb.shape), *cast_specs],
        out_specs=(row(D_MODEL), *cast_specs),
        compiler_params=pltpu.CompilerParams(
            dimension_semantics=("arbitrary",), vmem_limit_bytes=VMEM_LIMIT_BYTES),
        name="out_proj",
    )(x, ya, yb, wa, wb, *cast_weights)


def kernel(x, positions, ffn1_norm_g, ffn1_w_gate, ffn1_w_up, ffn1_w_down, mix_norm_g, w_in, gmlp_v_norm_g, gmlp_w_s, gmlp_b_s, mla_q_norm_g, mla_w_q_up, mla_kv_norm_g, mla_w_kv_up, mla_q_head_g, mla_k_head_g, gmlp_out_g, mla_out_g, w_out, ffn2_norm_g, ffn2_w_gate, ffn2_w_up, ffn2_w_down):
    B, S, D = x.shape
    assert D == D_MODEL and S % GMLP_CHUNK == 0
    depth = ffn1_norm_g.shape[0]
    cos_tab, sin_tab = _rope_tables(positions)
    xf = x.reshape(B * S, D)
    for l in range(depth):
        xf = _ffn(xf, ffn1_norm_g[l], ffn1_w_gate[l], ffn1_w_up[l], ffn1_w_down[l])
        ya, q, k, vt = _mixer_front(
            xf, cos_tab, sin_tab, mix_norm_g[l], w_in[l], gmlp_v_norm_g[l], gmlp_w_s[l],
            gmlp_b_s[l], gmlp_out_g[l], mla_q_norm_g[l], mla_w_q_up[l], mla_kv_norm_g[l],
            mla_w_kv_up[l], mla_q_head_g[l], mla_k_head_g[l])
        yb = _attention(q, k, vt, mla_out_g[l], B, S)
        xf, wg2, wu2, wd2 = _out_proj(xf, ya, yb, w_out[l],
                                      (ffn2_w_gate[l], ffn2_w_up[l], ffn2_w_down[l]))
        xf = _ffn(xf, ffn2_norm_g[l], wg2, wu2, wd2)
    return xf.reshape(B, S, D)
```

```python
import functools

import jax
import jax.numpy as jnp
import numpy as np
from jax import lax
from jax.experimental import pallas as pl
from jax.experimental.pallas import tpu as pltpu

D_MODEL = 2048
GMLP_WIDTH = 1024
GMLP_GROUP_DIM = 128
GMLP_GROUPS = 8
GMLP_CHUNK = 128
V_HEAD_DIM = 128
MLA_HEADS = 8
QK_NOPE_DIM = 128
QK_ROPE_DIM = 64
QK_HEAD_DIM = 192
Q_LORA_RANK = 512
KV_LORA_RANK = 512
ROPE_THETA = 10000.0
D_FF = 5504
EPS = 1e-6

LANES = 128
BF16_SUBLANES = 16
QK_PAD_DIM = 2 * LANES
FF_TILE = 768
FF_MAIN_STEPS = D_FF // FF_TILE
FF_TAIL = D_FF - FF_MAIN_STEPS * FF_TILE
assert FF_TAIL == LANES
FFN_ROW_TILE = 1024
MIXER_ROW_TILE = 512
MIXER_SUB_TILE = 256
OUT_PROJ_ROW_TILE = 512
ATTN_Q_TILE = 256
ATTN_SCORES_AHEAD = 3
VMEM_LIMIT_BYTES = 60 * 1024 * 1024
LOG2E = 1.4426950408889634

F32 = jnp.float32
BF16 = jnp.bfloat16


def _row_tile(n_rows, want):
    t = min(want, n_rows)
    assert n_rows % t == 0, (n_rows, t)
    return t


def _const_spec(shape):
    nd = len(shape)
    return pl.BlockSpec(shape, lambda *_: (0,) * nd, pipeline_mode=pl.Buffered(1))


def _rms_scale(x, width):
    ss = jnp.sum(x * x, axis=-1, keepdims=True)
    return lax.rsqrt(ss * (1.0 / width) + EPS)


def _rope_table_kernel(pos_ref, invf_ref, cos_ref, sin_ref):
    half = QK_ROPE_DIM // 2
    ang = invf_ref[...] * pos_ref[...].astype(F32)
    c = jnp.cos(ang)
    s = jnp.sin(ang)
    zeros = jnp.zeros((LANES - 2 * half, ang.shape[1]), F32)
    for ref, t in ((cos_ref, c), (sin_ref, s)):
        ref[0:half, :] = t
        ref[half:2 * half, :] = t
        ref[2 * half:, :] = zeros


def _rope_tables(positions):
    half = QK_ROPE_DIM // 2
    n_tok = positions.size
    inv_freq = 1.0 / (ROPE_THETA ** (jnp.arange(half, dtype=F32) / half))
    tk = _row_tile(n_tok, 2048)
    return pl.pallas_call(
        _rope_table_kernel,
        out_shape=(jax.ShapeDtypeStruct((LANES, n_tok), F32),) * 2,
        grid=(n_tok // tk,),
        in_specs=[pl.BlockSpec((1, tk), lambda i: (0, i)),
                  pl.BlockSpec((half, 1), lambda i: (0, 0))],
        out_specs=(pl.BlockSpec((LANES, tk), lambda i: (0, i)),) * 2,
        name="rope_tables",
    )(positions.reshape(1, n_tok), inv_freq.reshape(half, 1))


def _ffn_kernel(x_hbm, g_ref, wg_ref, wu_ref, wd_ref, wgu_tail_ref, wd_tail_ref,
                o_ref, h_ref, x_buf, x_sem):
    i = pl.program_id(0)
    j = pl.program_id(1)
    tm = o_ref.shape[0]

    def x_copy(tile):
        rows = pl.ds(pl.multiple_of(tile * tm, tm), tm)
        return pltpu.make_async_copy(x_hbm.at[rows, :], x_buf, x_sem)

    @pl.when(j == 0)
    def _():
        @pl.when(i == 0)
        def _():
            x_copy(0).start()

        x_copy(i).wait()
        x = x_buf[...]
        h_ref[...] = (x * _rms_scale(x, D_MODEL) * g_ref[...]).astype(BF16)
        o_ref[...] = x

    @pl.when((j == 1) & (i + 1 < pl.num_programs(0)))
    def _():
        x_copy(i + 1).start()

    def swiglu_half(gate, up):
        return ((gate / (1.0 + jnp.exp(-gate))) * (up * 0.5)).astype(BF16)

    def step(with_tail):
        h = h_ref[...]
        gate = jnp.dot(h, wg_ref[...], preferred_element_type=F32)
        up = jnp.dot(h, wu_ref[...], preferred_element_type=F32)
        if with_tail:
            gu = jnp.dot(h, wgu_tail_ref[...], preferred_element_type=F32)
        o_ref[...] += jnp.dot(swiglu_half(gate, up), wd_ref[...], preferred_element_type=F32)
        if with_tail:
            o_ref[...] += jnp.dot(swiglu_half(gu[:, :FF_TAIL], gu[:, FF_TAIL:]), wd_tail_ref[...],
                                  preferred_element_type=F32)

    @pl.when(j < FF_MAIN_STEPS - 1)
    def _():
        step(False)

    @pl.when(j == FF_MAIN_STEPS - 1)
    def _():
        step(True)


def _ffn(x, norm_g, w_gate, w_up, w_down):
    n_tok = x.shape[0]
    tm = _row_tile(n_tok, FFN_ROW_TILE)
    wg = w_gate.astype(BF16)
    wu = w_up.astype(BF16)
    wd = w_down.astype(BF16)
    main = FF_MAIN_STEPS * FF_TILE
    wgu_tail = jnp.concatenate([wg[:, main:], wu[:, main:]], axis=1)
    wd_tail = wd[main:]
    return pl.pallas_call(
        _ffn_kernel,
        out_shape=jax.ShapeDtypeStruct((n_tok, D_MODEL), F32),
        grid=(n_tok // tm, FF_MAIN_STEPS),
        in_specs=[pl.BlockSpec(memory_space=pl.ANY),
                  pl.BlockSpec((1, D_MODEL), lambda i, j: (0, 0)),
                  pl.BlockSpec((D_MODEL, FF_TILE), lambda i, j: (0, j)),
                  pl.BlockSpec((D_MODEL, FF_TILE), lambda i, j: (0, j)),
                  pl.BlockSpec((FF_TILE, D_MODEL), lambda i, j: (j, 0)),
                  _const_spec(wgu_tail.shape), _const_spec(wd_tail.shape)],
        out_specs=pl.BlockSpec((tm, D_MODEL), lambda i, j: (i, 0)),
        scratch_shapes=[pltpu.VMEM((tm, D_MODEL), BF16),
                        pltpu.VMEM((tm, D_MODEL), F32),
                        pltpu.SemaphoreType.DMA(())],
        compiler_params=pltpu.CompilerParams(
            dimension_semantics=("arbitrary", "arbitrary"),
            vmem_limit_bytes=VMEM_LIMIT_BYTES),
        name="ffn",
    )(x, norm_g.reshape(1, D_MODEL), wg, wu, wd, wgu_tail, wd_tail)


def _rope(r, cos_t, sin_t, lane):
    half = QK_ROPE_DIM // 2
    rot = jnp.where(lane < half, -pltpu.roll(r, LANES - half, 1), pltpu.roll(r, half, 1))
    return r * cos_t + rot * sin_t


def _mixer_front_kernel(x_ref, *refs):
    consts, (cos_ref, sin_ref, ya_ref, q_ref, k_ref, vt_ref) = refs[:-6], refs[-6:]
    ws_ref, bs_ref, og_ref = consts[7:10]
    sub = min(MIXER_SUB_TILE, x_ref.shape[0])
    pending = []
    for r in range(0, x_ref.shape[0], sub):
        rows = pl.ds(r, sub)
        zu, vn = _mixer_front_rows(
            x_ref.at[rows, :], *consts, cos_ref.at[:, rows], sin_ref.at[:, rows],
            q_ref.at[rows, :], k_ref.at[rows, :], vt_ref.at[:, rows])
        pending.append((zu, vn, ya_ref.at[rows, :]))
    for zu, vn, ya_rows in pending:
        _spatial_gating(zu, vn, ws_ref, bs_ref, og_ref, ya_rows)


def _mixer_front_rows(x_ref, mixg_ref, wu_ref, wv_ref, wcq_ref, wckv_ref, wkr_ref,
                      vng_ref, ws_ref, bs_ref, og_ref,
                      qng_ref, wqup_ref, kvng_ref, wkn_ref, wvv_ref, qhg_ref, khg_ref,
                      cos_ref, sin_ref, q_ref, k_ref, vt_ref):
    tm = x_ref.shape[0]
    x = x_ref[...]
    h = (x * _rms_scale(x, D_MODEL) * mixg_ref[...]).astype(BF16)

    cq = jnp.dot(h, wcq_ref[...], preferred_element_type=F32)
    ckv = jnp.dot(h, wckv_ref[...], preferred_element_type=F32)
    kr = jnp.dot(h, wkr_ref[...], preferred_element_type=F32)
    zv = jnp.dot(h, wv_ref[...], preferred_element_type=F32)
    cqn = (cq * _rms_scale(cq, Q_LORA_RANK) * qng_ref[...]).astype(BF16)
    ckvn = (ckv * _rms_scale(ckv, KV_LORA_RANK) * kvng_ref[...]).astype(BF16)
    qf = jnp.dot(cqn, wqup_ref[...], preferred_element_type=F32)
    kn_all = jnp.dot(ckvn, wkn_ref[...], preferred_element_type=F32)
    vt_ref[...] = jnp.dot(ckvn, wvv_ref[...], preferred_element_type=F32).T.astype(BF16)
    zu = jnp.dot(h, wu_ref[...], preferred_element_type=F32)

    sqrt_half = np.float32(np.sqrt(0.5))

    def gelu(z):
        return 0.5 * z * (1.0 + lax.erf(z * sqrt_half))

    zu = gelu(zu)
    zv = gelu(zv)
    vn = (zv * _rms_scale(zv, GMLP_WIDTH) * vng_ref[...]).astype(BF16)

    cos_t = cos_ref[...].T
    sin_t = sin_ref[...].T
    lane = lax.broadcasted_iota(jnp.int32, (tm, LANES), 1)
    scale = np.float32(QK_HEAD_DIM ** -0.5 * LOG2E)
    qhg_n = qhg_ref[:, :LANES]
    qhg_r = qhg_ref[:, LANES:]
    for hd in range(MLA_HEADS):
        base = hd * QK_PAD_DIM
        qn = qf[:, base:base + LANES]
        qr = qf[:, base + LANES:base + QK_PAD_DIM]
        ss = jnp.sum(qn * qn + qr * qr, axis=-1, keepdims=True)
        inv = lax.rsqrt(ss * (1.0 / QK_HEAD_DIM) + EPS) * scale
        q_ref[:, base:base + LANES] = (qn * inv * qhg_n).astype(BF16)
        q_ref[:, base + LANES:base + QK_PAD_DIM] = _rope(
            qr * inv * qhg_r, cos_t, sin_t, lane).astype(BF16)

    kr_ss = jnp.sum(kr * kr, axis=-1, keepdims=True)
    kr_rot = _rope(kr * khg_ref[:, LANES:], cos_t, sin_t, lane)
    khg_n = khg_ref[:, :LANES]
    for hd in range(MLA_HEADS):
        base = hd * QK_PAD_DIM
        kn = kn_all[:, hd * LANES:(hd + 1) * LANES]
        ss = jnp.sum(kn * kn, axis=-1, keepdims=True) + kr_ss
        inv = lax.rsqrt(ss * (1.0 / QK_HEAD_DIM) + EPS)
        k_ref[:, base:base + LANES] = (kn * inv * khg_n).astype(BF16)
        k_ref[:, base + LANES:base + QK_PAD_DIM] = (kr_rot * inv).astype(BF16)
    return zu, vn


def _spatial_gating(zu, vn, ws_ref, bs_ref, og_ref, ya_ref):
    t_idx = lax.broadcasted_iota(jnp.int32, (GMLP_CHUNK, GMLP_CHUNK), 0)
    s_idx = lax.broadcasted_iota(jnp.int32, (GMLP_CHUNK, GMLP_CHUNK), 1)
    causal = t_idx >= s_idx
    for g in range(GMLP_GROUPS):
        cols = slice(g * GMLP_GROUP_DIM, (g + 1) * GMLP_GROUP_DIM)
        w_causal = jnp.where(causal, ws_ref[g], 0.0).astype(BF16)
        for c in range(zu.shape[0] // GMLP_CHUNK):
            rows = slice(c * GMLP_CHUNK, (c + 1) * GMLP_CHUNK)
            mixed = jnp.dot(w_causal, vn[rows, cols], preferred_element_type=F32) + bs_ref[g]
            y = zu[rows, cols] * mixed
            y = y * _rms_scale(y, GMLP_GROUP_DIM) * og_ref[:, cols]
            ya_ref[rows, cols] = y.astype(BF16)


def _mixer_front(x, cos_tab, sin_tab, mix_norm_g, w_in, v_norm_g, w_s, b_s, out_g,
                 q_norm_g, w_q_up, kv_norm_g, w_kv_up, q_head_g, k_head_g):
    n_tok = x.shape[0]
    tm = _row_tile(n_tok, MIXER_ROW_TILE)
    H = MLA_HEADS
    c0 = GMLP_WIDTH
    c1 = 2 * GMLP_WIDTH
    c2 = c1 + Q_LORA_RANK
    c3 = c2 + KV_LORA_RANK
    wu = w_in[:, :c0].astype(BF16)
    wv = w_in[:, c0:c1].astype(BF16)
    wcq = w_in[:, c1:c2].astype(BF16)
    wckv = w_in[:, c2:c3].astype(BF16)
    wkr = jnp.pad(w_in[:, c3:], ((0, 0), (0, LANES - QK_ROPE_DIM))).astype(BF16)
    zpad = QK_PAD_DIM - QK_HEAD_DIM
    wqup = jnp.pad(w_q_up.reshape(Q_LORA_RANK, H, QK_HEAD_DIM), ((0, 0), (0, 0), (0, zpad)))
    wqup = wqup.reshape(Q_LORA_RANK, H * QK_PAD_DIM).astype(BF16)
    wkv = w_kv_up.reshape(KV_LORA_RANK, H, QK_NOPE_DIM + V_HEAD_DIM)
    wkn = wkv[:, :, :QK_NOPE_DIM].reshape(KV_LORA_RANK, H * QK_NOPE_DIM).astype(BF16)
    wvv = wkv[:, :, QK_NOPE_DIM:].reshape(KV_LORA_RANK, H * V_HEAD_DIM).astype(BF16)
    qhg = jnp.pad(q_head_g, (0, zpad)).reshape(1, QK_PAD_DIM)
    khg = jnp.pad(k_head_g, (0, zpad)).reshape(1, QK_PAD_DIM)
    bs_full = jnp.broadcast_to(b_s[:, :, None], (GMLP_GROUPS, GMLP_CHUNK, GMLP_GROUP_DIM))

    row = lambda w: pl.BlockSpec((tm, w), lambda i: (i, 0))
    col = lambda h: pl.BlockSpec((h, tm), lambda i: (0, i))
    consts = [mix_norm_g.reshape(1, D_MODEL), wu, wv, wcq, wckv, wkr,
              v_norm_g.reshape(1, GMLP_WIDTH), w_s, bs_full, out_g.reshape(1, GMLP_WIDTH),
              q_norm_g.reshape(1, Q_LORA_RANK), wqup, kv_norm_g.reshape(1, KV_LORA_RANK),
              wkn, wvv, qhg, khg]
    return pl.pallas_call(
        _mixer_front_kernel,
        out_shape=(jax.ShapeDtypeStruct((n_tok, GMLP_WIDTH), BF16),
                   jax.ShapeDtypeStruct((n_tok, H * QK_PAD_DIM), BF16),
                   jax.ShapeDtypeStruct((n_tok, H * QK_PAD_DIM), BF16),
                   jax.ShapeDtypeStruct((H * V_HEAD_DIM, n_tok), BF16)),
        grid=(n_tok // tm,),
        in_specs=[row(D_MODEL)] + [_const_spec(a.shape) for a in consts] + [col(LANES), col(LANES)],
        out_specs=(row(GMLP_WIDTH), row(H * QK_PAD_DIM), row(H * QK_PAD_DIM), col(H * V_HEAD_DIM)),
        compiler_params=pltpu.CompilerParams(
            dimension_semantics=("parallel",), vmem_limit_bytes=VMEM_LIMIT_BYTES),
        name="mixer_front",
    )(x, *consts, cos_tab, sin_tab)


def _attn_kernel(q_ref, k_ref, vt_ref, g_ref, o_ref):
    seq = q_ref.shape[0]
    tq = min(ATTN_Q_TILE, seq)
    nt = (((1,), (1,)), ((), ()))
    key_i = lax.broadcasted_iota(jnp.int32, (tq, tq), 0)
    qry_i = lax.broadcasted_iota(jnp.int32, (tq, tq), 1)
    causal = key_i <= qry_i
    g = g_ref[...]
    n_tiles = seq // tq

    def scores(i):
        lo, hi = i * tq, (i + 1) * tq
        q = q_ref[lo:hi, :]
        s_d = lax.dot_general(k_ref[lo:hi, :], q, nt, preferred_element_type=F32)
        s_d = jnp.where(causal, s_d, F32(-1e30))
        s_o = lax.dot_general(k_ref[0:lo, :], q, nt, preferred_element_type=F32) if i > 0 else None
        return s_d, s_o

    def finish(i, s_d, s_o):
        lo, hi = i * tq, (i + 1) * tq
        m = jnp.max(s_d, axis=0, keepdims=True)
        if i > 0:
            m = jnp.maximum(m, jnp.max(s_o, axis=0, keepdims=True))
        p_d = jnp.exp2(s_d - m)
        l = jnp.sum(p_d, axis=0, keepdims=True)
        o_t = jnp.dot(vt_ref[:, lo:hi], p_d.astype(BF16), preferred_element_type=F32)
        if i > 0:
            p_o = jnp.exp2(s_o - m)
            l = l + jnp.sum(p_o, axis=0, keepdims=True)
            o_t = o_t + jnp.dot(vt_ref[:, 0:lo], p_o.astype(BF16), preferred_element_type=F32)
        o_t = o_t / l
        ms = jnp.sum(o_t * o_t, axis=0, keepdims=True) * (1.0 / V_HEAD_DIM)
        o_ref[lo:hi, :] = ((o_t * lax.rsqrt(ms + EPS)).T * g).astype(BF16)

    order = list(range(n_tiles - 1, -1, -1))
    ahead = [scores(i) for i in order[:ATTN_SCORES_AHEAD]]
    for n, i in enumerate(order):
        if n + ATTN_SCORES_AHEAD < n_tiles:
            ahead.append(scores(order[n + ATTN_SCORES_AHEAD]))
        finish(i, *ahead.pop(0))


def _attention(q, k, vt, out_g, batch, seq):
    H = MLA_HEADS
    return pl.pallas_call(
        _attn_kernel,
        out_shape=jax.ShapeDtypeStruct((batch * seq, H * V_HEAD_DIM), BF16),
        grid=(batch, H),
        in_specs=[pl.BlockSpec((seq, QK_PAD_DIM), lambda b, h: (b, h)),
                  pl.BlockSpec((seq, QK_PAD_DIM), lambda b, h: (b, h)),
                  pl.BlockSpec((V_HEAD_DIM, seq), lambda b, h: (h, b)),
                  pl.BlockSpec((1, V_HEAD_DIM), lambda b, h: (0, h))],
        out_specs=pl.BlockSpec((seq, V_HEAD_DIM), lambda b, h: (b, h)),
        compiler_params=pltpu.CompilerParams(
            dimension_semantics=("parallel", "parallel"),
            vmem_limit_bytes=VMEM_LIMIT_BYTES),
        name="attention",
    )(q, k, vt, out_g.reshape(1, H * V_HEAD_DIM))


def _out_proj_kernel(x_ref, ya_ref, yb_ref, wa_ref, wb_ref, *rest):
    n_cast = (len(rest) - 1) // 2
    cast_in, o_ref, cast_out = rest[:n_cast], rest[n_cast], rest[n_cast + 1:]
    o_ref[...] = (x_ref[...]
                  + jnp.dot(ya_ref[...], wa_ref[...], preferred_element_type=F32)
                  + jnp.dot(yb_ref[...], wb_ref[...], preferred_element_type=F32))
    for src, dst in zip(cast_in, cast_out):
        dst[...] = src[...].astype(BF16)


def _cast_slab_spec(n_rows, n_cols, n_steps):
    for unit in (BF16_SUBLANES, LANES):
        rows = min(-(-n_rows // (n_steps * unit)) * unit, n_rows)
        if n_rows % rows == 0:
            break
    assert n_rows % rows == 0, (n_rows, n_steps)
    last = n_rows // rows - 1
    return pl.BlockSpec((rows, n_cols), lambda i: (jnp.minimum(i, last), 0))


def _out_proj(x, ya, yb, w_out, cast_weights):
    n_tok = x.shape[0]
    tm = _row_tile(n_tok, OUT_PROJ_ROW_TILE)
    n_steps = n_tok // tm
    wa = w_out[:GMLP_WIDTH].astype(BF16)
    wb = w_out[GMLP_WIDTH:].astype(BF16)
    row = lambda w: pl.BlockSpec((tm, w), lambda i: (i, 0))
    cast_specs = [_cast_slab_spec(*w.shape, n_steps) for w in cast_weights]
    return pl.pallas_call(
        _out_proj_kernel,
        out_shape=(jax.ShapeDtypeStruct((n_tok, D_MODEL), F32),
                   *[jax.ShapeDtypeStruct(w.shape, BF16) for w in cast_weights]),
        grid=(n_steps,),
        in_specs=[row(D_MODEL), row(GMLP_WIDTH), row(MLA_HEADS * V_HEAD_DIM),
                  _const_spec(wa.shape), _const_spec(wb.shape), *cast_specs],
        out_specs=(row(D_MODEL), *cast_specs),
        compiler_params=pltpu.CompilerParams(
            dimension_semantics=("arbitrary",), vmem_limit_bytes=VMEM_LIMIT_BYTES),
        name="out_proj",
    )(x, ya, yb, wa, wb, *cast_weights)


def kernel(x, positions, ffn1_norm_g, ffn1_w_gate, ffn1_w_up, ffn1_w_down, mix_norm_g, w_in, gmlp_v_norm_g, gmlp_w_s, gmlp_b_s, mla_q_norm_g, mla_w_q_up, mla_kv_norm_g, mla_w_kv_up, mla_q_head_g, mla_k_head_g, gmlp_out_g, mla_out_g, w_out, ffn2_norm_g, ffn2_w_gate, ffn2_w_up, ffn2_w_down):
    B, S, D = x.shape
    assert D == D_MODEL and S % GMLP_CHUNK == 0
    depth = ffn1_norm_g.shape[0]
    cos_tab, sin_tab = _rope_tables(positions)
    xf = x.reshape(B * S, D)
    for l in range(depth):
        xf = _ffn(xf, ffn1_norm_g[l], ffn1_w_gate[l], ffn1_w_up[l], ffn1_w_down[l])
        ya, q, k, vt = _mixer_front(
            xf, cos_tab, sin_tab, mix_norm_g[l], w_in[l], gmlp_v_norm_g[l], gmlp_w_s[l],
            gmlp_b_s[l], gmlp_out_g[l], mla_q_norm_g[l], mla_w_q_up[l], mla_kv_norm_g[l],
            mla_w_kv_up[l], mla_q_head_g[l], mla_k_head_g[l])
        yb = _attention(q, k, vt, mla_out_g[l], B, S)
        xf, wg2, wu2, wd2 = _out_proj(xf, ya, yb, w_out[l],
                                      (ffn2_w_gate[l], ffn2_w_up[l], ffn2_w_down[l]))
        xf = _ffn(xf, ffn2_norm_g[l], wg2, wu2, wd2)
    return xf.reshape(B, S, D)
```
